```python
import math
import jax, jax.numpy as jnp
from jax import lax
import numpy as np

D_MODEL = 1024
BATCH = 8
SEQ = 2048
DEPTH = 2
DEC_BATCH = 128
DEC_SEQ = 4
PAST_LEN = 16384
PAGE_SIZE = 128

D_RNN = D_MODEL
LRU_HEADS = 16
LRU_BLOCK = D_RNN // LRU_HEADS
CONV_W = 4
LRU_C = 8.0
D_S5 = D_MODEL
S5_GROUP = 16
S5_GROUPS = D_S5 // S5_GROUP
S5_STATE = 64
S5_CHUNK = 128
D_FF = int(math.ceil(8 * D_MODEL / 3 / 256)) * 256
DN_ALPHA = (2 * DEPTH) ** 0.25
DN_BETA = (8 * DEPTH) ** -0.25
LN_EPS = 1e-5
N_ADA = 6
SPLITS = [D_RNN, 2 * D_RNN, 2 * D_RNN + D_S5, 2 * D_RNN + D_S5 + D_MODEL]
N_IN = 2 * D_RNN + D_S5 + 2 * D_MODEL

kernel_name = "hawk_s5_gated_hybrid_step"


def layer_norm(x, g, b):
    xf = x.astype(jnp.float32)
    mu = xf.mean(-1, keepdims=True)
    var = jnp.square(xf - mu).mean(-1, keepdims=True)
    y = (xf - mu) * lax.rsqrt(var + LN_EPS) * g.astype(jnp.float32) + b.astype(jnp.float32)
    return y.astype(x.dtype)


def causal_conv(u, buf, w, b):
    T = u.shape[1]
    full = jnp.concatenate([buf.astype(u.dtype), u], axis=1)
    y = b + sum(full[:, k:k + T] * w[k] for k in range(CONV_W))
    return y, full[:, -(CONV_W - 1):]


def _lin_op(l, r):
    a1, b1 = l
    a2, b2 = r
    return a1 * a2, a2 * b1 + b2


def _clin_op(l, r):
    a1r, a1i, b1r, b1i = l
    a2r, a2i, b2r, b2i = r
    return (a2r * a1r - a2i * a1i, a2r * a1i + a2i * a1r,
            a2r * b1r - a2i * b1i + b2r, a2r * b1i + a2i * b1r + b2i)


def rg_lru(v, h0, w_gates, b_gates, lam):
    Bn, T, _ = v.shape
    vb = v.reshape(Bn, T, LRU_HEADS, LRU_BLOCK)
    g = jnp.einsum('bthi,ghij->gbthj', vb, w_gates).reshape(2, Bn, T, D_RNN)
    gates = jax.nn.sigmoid(g.astype(jnp.float32) + b_gates.astype(jnp.float32)[:, None, None, :])
    r, i = gates[0], gates[1]
    log_a = -LRU_C * r * jax.nn.softplus(-lam.astype(jnp.float32))
    a = jnp.exp(log_a)
    bterm = jnp.sqrt(-jnp.expm1(2.0 * log_a)) * (i * v.astype(jnp.float32))
    bterm = bterm.at[:, 0].add(a[:, 0] * h0.astype(jnp.float32))
    _, h = lax.associative_scan(_lin_op, (a, bterm), axis=1)
    return h, h[:, -1]


def s5_discretise(a_re, a_im, log_dt, b_re, b_im):
    a_re = a_re.astype(jnp.float32)
    a_im = a_im.astype(jnp.float32)
    b_re = b_re.astype(jnp.float32)
    b_im = b_im.astype(jnp.float32)
    dt = jnp.exp(log_dt.astype(jnp.float32))[:, None]
    mag = jnp.exp(a_re * dt)
    abar_re, abar_im = mag * jnp.cos(a_im * dt), mag * jnp.sin(a_im * dt)
    nr, ni = abar_re - 1.0, abar_im
    den = a_re * a_re + a_im * a_im
    fr = (nr * a_re + ni * a_im) / den
    fi = (ni * a_re - nr * a_im) / den
    bb_re = fr[..., None] * b_re - fi[..., None] * b_im
    bb_im = fr[..., None] * b_im + fi[..., None] * b_re
    return abar_re, abar_im, bb_re, bb_im


def s5_scan(u, h0_re, h0_im, a_re, a_im, bb_re, bb_im, c_re, c_im):
    Bn, T, G, Q = u.shape
    L = math.gcd(T, S5_CHUNK)
    n = T // L
    uc = u.reshape(Bn, n, L, G, Q).swapaxes(0, 1)
    c_re = c_re.astype(jnp.float32)
    c_im = c_im.astype(jnp.float32)

    def body(carry, u_blk):
        hr, hi = carry
        br = jnp.einsum('blgq,gpq->blgp', u_blk, bb_re)
        bi = jnp.einsum('blgq,gpq->blgp', u_blk, bb_im)
        br = br.at[:, 0].add(a_re * hr - a_im * hi)
        bi = bi.at[:, 0].add(a_re * hi + a_im * hr)
        ar = jnp.broadcast_to(a_re, br.shape)
        ai = jnp.broadcast_to(a_im, bi.shape)
        _, _, xr, xi = lax.associative_scan(_clin_op, (ar, ai, br, bi), axis=1)
        y = jnp.einsum('blgp,gqp->blgq', xr, c_re) - jnp.einsum('blgp,gqp->blgq', xi, c_im)
        return (xr[:, -1], xi[:, -1]), y

    (hr, hi), ys = lax.scan(body, (h0_re.astype(jnp.float32), h0_im.astype(jnp.float32)), uc)
    return ys.swapaxes(0, 1).reshape(Bn, T, G, Q), hr, hi


def _layer(x, c, h0, conv0, sr0, si0, w_ada, b_ada, w_in, b_in, w_conv, b_conv,
           w_lru_gates, b_lru_gates, lru_lambda, w_lru_out, s5_a_re, s5_a_im, s5_log_dt,
           s5_b_re, s5_b_im, s5_c_re, s5_c_im, s5_d, w_s5_glu, w_out, ln1_g, ln1_b,
           w_ffn_up, w_ffn_down, ln2_g, ln2_b):
    dt = x.dtype
    Bn, T, _ = x.shape
    ada = (jax.nn.silu(c) @ w_ada + b_ada).reshape(Bn, N_ADA, D_MODEL)[:, :, None, :]
    sh1, sc1, g1, sh2, sc2, g2 = (ada[:, k] for k in range(N_ADA))
    u = x * (1 + sc1) + sh1
    z = u @ w_in + b_in
    z_lru, z_gate, z_s5, m_a, m_b = jnp.split(z, SPLITS, axis=-1)
    v, conv_new = causal_conv(z_lru, conv0, w_conv, b_conv)
    h, h_last = rg_lru(v, h0, w_lru_gates, b_lru_gates, lru_lambda)
    br_a = (h.astype(dt) * jax.nn.gelu(z_gate)) @ w_lru_out
    abr, abi, bbr, bbi = s5_discretise(s5_a_re, s5_a_im, s5_log_dt, s5_b_re, s5_b_im)
    us = z_s5.astype(jnp.float32).reshape(Bn, T, S5_GROUPS, S5_GROUP)
    ys, sr, si = s5_scan(us, sr0, si0, abr, abi, bbr, bbi, s5_c_re, s5_c_im)
    ys = ys.reshape(Bn, T, D_S5) + s5_d.astype(jnp.float32) * z_s5.astype(jnp.float32)
    gl = jax.nn.gelu(ys.astype(dt)) @ w_s5_glu
    br_b = gl[..., :D_MODEL] * jax.nn.sigmoid(gl[..., D_MODEL:])
    mixed = (jax.nn.sigmoid(m_a) * br_a + jax.nn.sigmoid(m_b) * br_b) @ w_out
    x = layer_norm(DN_ALPHA * x + g1 * mixed, ln1_g, ln1_b)
    u2 = x * (1 + sc2) + sh2
    hf = u2 @ w_ffn_up
    f = (jax.nn.silu(hf[..., :D_FF]) * hf[..., D_FF:]) @ w_ffn_down
    x = layer_norm(DN_ALPHA * x + g2 * f, ln2_g, ln2_b)
    return x, h_last.astype(dt), conv_new.astype(dt), sr.astype(dt), si.astype(dt)


def setup_inputs(seed: int = 0) -> dict:
    key = jax.random.key(seed)
    ks = iter(jax.random.split(key, 48))
    nrm = lambda shape, s: jax.random.normal(next(ks), shape, jnp.float32) * s
    d = {}
    d['x_prompt'] = nrm((BATCH, SEQ, D_MODEL), 1.0)
    d['x_sample'] = nrm((DEC_BATCH, DEC_SEQ, D_MODEL), 1.0)
    d['c_prompt'] = nrm((BATCH, D_MODEL), 1.0)
    d['c_sample'] = nrm((DEC_BATCH, D_MODEL), 1.0)
    d['state_lru_h'] = nrm((DEPTH, DEC_BATCH, D_RNN), 0.5)
    d['state_lru_conv'] = nrm((DEPTH, DEC_BATCH, CONV_W - 1, D_RNN), 1.0)
    d['state_s5_re'] = nrm((DEPTH, DEC_BATCH, S5_GROUPS, S5_STATE), 0.1)
    d['state_s5_im'] = nrm((DEPTH, DEC_BATCH, S5_GROUPS, S5_STATE), 0.1)
    d['w_ada'] = nrm((DEPTH, D_MODEL, N_ADA * D_MODEL), D_MODEL ** -0.5)
    d['b_ada'] = nrm((DEPTH, N_ADA * D_MODEL), 0.01)
    d['w_in'] = nrm((DEPTH, D_MODEL, N_IN), D_MODEL ** -0.5)
    d['b_in'] = nrm((DEPTH, N_IN), 0.01)
    d['w_conv'] = nrm((DEPTH, CONV_W, D_RNN), CONV_W ** -0.5)
    d['b_conv'] = nrm((DEPTH, D_RNN), 0.01)
    d['w_lru_gates'] = nrm((DEPTH, 2, LRU_HEADS, LRU_BLOCK, LRU_BLOCK), LRU_BLOCK ** -0.5)
    d['b_lru_gates'] = nrm((DEPTH, 2, D_RNN), 0.01)
    a_pow = jax.random.uniform(next(ks), (DEPTH, D_RNN), jnp.float32, 0.9, 0.999)
    s = a_pow ** (1.0 / LRU_C)
    d['lru_lambda'] = jnp.log(s) - jnp.log1p(-s)
    d['w_lru_out'] = nrm((DEPTH, D_RNN, D_MODEL), D_RNN ** -0.5)
    d['s5_a_re'] = -0.5 + nrm((DEPTH, S5_GROUPS, S5_STATE), 0.01)
    d['s5_a_im'] = jnp.pi * jnp.arange(S5_STATE, dtype=jnp.float32) + nrm((DEPTH, S5_GROUPS, S5_STATE), 0.01)
    d['s5_log_dt'] = jax.random.uniform(next(ks), (DEPTH, S5_GROUPS), jnp.float32, math.log(0.001), math.log(0.1))
    d['s5_b_re'] = nrm((DEPTH, S5_GROUPS, S5_STATE, S5_GROUP), (2 * S5_GROUP) ** -0.5)
    d['s5_b_im'] = nrm((DEPTH, S5_GROUPS, S5_STATE, S5_GROUP), (2 * S5_GROUP) ** -0.5)
    d['s5_c_re'] = nrm((DEPTH, S5_GROUPS, S5_GROUP, S5_STATE), (2 * S5_STATE) ** -0.5)
    d['s5_c_im'] = nrm((DEPTH, S5_GROUPS, S5_GROUP, S5_STATE), (2 * S5_STATE) ** -0.5)
    d['s5_d'] = nrm((DEPTH, D_S5), 1.0)
    d['w_s5_glu'] = nrm((DEPTH, D_S5, 2 * D_MODEL), D_S5 ** -0.5)
    d['w_out'] = nrm((DEPTH, D_MODEL, D_MODEL), D_MODEL ** -0.5 * DN_BETA)
    d['ln1_g'] = 1.0 + nrm((DEPTH, D_MODEL), 0.02)
    d['ln1_b'] = nrm((DEPTH, D_MODEL), 0.01)
    d['w_ffn_up'] = nrm((DEPTH, D_MODEL, 2 * D_FF), D_MODEL ** -0.5)
    d['w_ffn_down'] = nrm((DEPTH, D_FF, D_MODEL), D_FF ** -0.5 * DN_BETA)
    d['ln2_g'] = 1.0 + nrm((DEPTH, D_MODEL), 0.02)
    d['ln2_b'] = nrm((DEPTH, D_MODEL), 0.01)
    return d


def reference(x_prompt, x_sample, c_prompt, c_sample, state_lru_h, state_lru_conv, state_s5_re,
              state_s5_im, w_ada, b_ada, w_in, b_in, w_conv, b_conv, w_lru_gates, b_lru_gates,
              lru_lambda, w_lru_out, s5_a_re, s5_a_im, s5_log_dt, s5_b_re, s5_b_im, s5_c_re,
              s5_c_im, s5_d, w_s5_glu, w_out, ln1_g, ln1_b, w_ffn_up, w_ffn_down, ln2_g, ln2_b):
    dt = x_prompt.dtype
    Bp = x_prompt.shape[0]
    hp0 = jnp.zeros((Bp, D_RNN), dt)
    cp0 = jnp.zeros((Bp, CONV_W - 1, D_RNN), dt)
    sp0 = jnp.zeros((Bp, S5_GROUPS, S5_STATE), dt)
    yp, ys = x_prompt, x_sample
    hp, hs, cvp, cvs, srp, srs, sip, sis = [], [], [], [], [], [], [], []
    for l in range(DEPTH):
        lw = (w_ada[l], b_ada[l], w_in[l], b_in[l], w_conv[l], b_conv[l], w_lru_gates[l],
              b_lru_gates[l], lru_lambda[l], w_lru_out[l], s5_a_re[l], s5_a_im[l], s5_log_dt[l],
              s5_b_re[l], s5_b_im[l], s5_c_re[l], s5_c_im[l], s5_d[l], w_s5_glu[l], w_out[l],
              ln1_g[l], ln1_b[l], w_ffn_up[l], w_ffn_down[l], ln2_g[l], ln2_b[l])
        yp, h1, c1, r1, i1 = _layer(yp, c_prompt, hp0, cp0, sp0, sp0, *lw)
        ys, h2, c2, r2, i2 = _layer(ys, c_sample, state_lru_h[l], state_lru_conv[l],
                                    state_s5_re[l], state_s5_im[l], *lw)
        hp.append(h1); hs.append(h2); cvp.append(c1); cvs.append(c2)
        srp.append(r1); srs.append(r2); sip.append(i1); sis.append(i2)
    return (yp, ys, jnp.stack(hp), jnp.stack(hs), jnp.stack(cvp), jnp.stack(cvs),
            jnp.stack(srp), jnp.stack(srs), jnp.stack(sip), jnp.stack(sis))
```

```python
import functools

import jax
import jax.numpy as jnp
from jax import lax
from jax.experimental import pallas as pl
from jax.experimental.pallas import tpu as pltpu

LRU_C = 8.0
LN_EPS = 1e-5
N_ADA = 6
F32 = jnp.float32
BF16 = jnp.bfloat16

SUBLANES = 8
MXU_DIM = 256
ROWS_PER_BLOCK = 512
VMEM_LIMIT = 60 * 1024 * 1024


def _resident(shape):
    nd = len(shape)
    return pl.BlockSpec(shape, lambda *_: (0,) * nd, pipeline_mode=pl.Buffered(1))


def _carried(shape):
    nd = len(shape)
    return pl.BlockSpec(shape, lambda *_: (0,) * nd)


def _layer_norm(x, g, b):
    mu = jnp.mean(x, axis=-1, keepdims=True)
    xc = x - mu
    var = jnp.mean(xc * xc, axis=-1, keepdims=True)
    return xc * lax.rsqrt(var + LN_EPS) * g + b


def _softplus(x):
    return jnp.maximum(x, 0.0) + jnp.log1p(jnp.exp(-jnp.abs(x)))


def _dot(a, b):
    return jnp.dot(a, b, preferred_element_type=F32)


def _ada_kernel(c_ref, w_ref, b_ref, o_ref):
    s = jax.nn.silu(c_ref[...]).astype(BF16)
    o_ref[...] = _dot(s, w_ref[...].astype(BF16)) + b_ref[...]


def _ada_call(c_all, w_ada, b_ada):
    depth, d, n = w_ada.shape
    m = c_all.shape[0]
    tn = d
    return pl.pallas_call(
        _ada_kernel,
        grid=(depth, n // tn),
        in_specs=[
            pl.BlockSpec((m, d), lambda l, j: (0, 0)),
            pl.BlockSpec((None, d, tn), lambda l, j: (l, 0, j)),
            pl.BlockSpec((None, 1, tn), lambda l, j: (l, 0, j)),
        ],
        out_specs=pl.BlockSpec((None, m, tn), lambda l, j: (l, 0, j)),
        out_shape=jax.ShapeDtypeStruct((depth, m, n), F32),
        compiler_params=pltpu.CompilerParams(
            dimension_semantics=("arbitrary", "arbitrary")),
        name="ada",
    )(c_all, w_ada, b_ada.reshape(depth, 1, n))


def _s5_disc_kernel(a_re_ref, a_im_ref, log_dt_ref, b_re_ref, b_im_ref,
                    abr_ref, abi_ref, bbr_ref, bbi_ref):
    a_re = a_re_ref[...]
    a_im = a_im_ref[...]
    dt = jnp.exp(log_dt_ref[...])
    mag = jnp.exp(a_re * dt)
    abr = mag * jnp.cos(a_im * dt)
    abi = mag * jnp.sin(a_im * dt)
    nr, ni = abr - 1.0, abi
    den = a_re * a_re + a_im * a_im
    fr = (nr * a_re + ni * a_im) / den
    fi = (ni * a_re - nr * a_im) / den
    b_re = b_re_ref[...]
    b_im = b_im_ref[...]
    abr_ref[...] = abr
    abi_ref[...] = abi
    bbr_ref[...] = fr * b_re - fi * b_im
    bbi_ref[...] = fr * b_im + fi * b_re


def _s5_discretise(a_re, a_im, log_dt, b_re, b_im):
    depth, g, p, q = b_re.shape
    shape2 = (depth * g, p * q)
    bc = lambda a: jnp.broadcast_to(a[..., None], (depth, g, p, q)).reshape(shape2)
    dt_b = jnp.broadcast_to(log_dt[..., None, None], (depth, g, p, q)).reshape(shape2)
    outs = pl.pallas_call(
        _s5_disc_kernel,
        out_shape=[jax.ShapeDtypeStruct(shape2, F32)] * 4,
        name="s5_discretise",
    )(bc(a_re), bc(a_im), dt_b, b_re.reshape(shape2), b_im.reshape(shape2))
    abr, abi, bbr, bbi = (o.reshape(depth, g, p, q) for o in outs)
    return abr[..., 0], abi[..., 0], bbr, bbi


def _scan_slabs(tb, nb, body, carry):
    if tb <= 8:
        for t in range(tb):
            carry = body(t * nb, carry)
        return carry

    def step(t, c):
        return body(pl.multiple_of(t * nb, nb), c)

    return lax.fori_loop(0, tb, step, carry, unroll=8)


def _mixer_kernel(tb, nb, conv_w, n_gate_tiles, n_s5_chunks, alpha,
                  x_ref, ada_ref, h0_ref, conv0_ref, sr0_ref, si0_ref,
                  w_in_ref, b_in_ref, w_conv_ref, b_conv_ref, wg_ref, bg_ref,
                  lam_ref, w_lo_ref, ar_ref, ai_ref, wb_ref, wc_ref, d_ref,
                  w_glu_ref, w_out_ref, lng_ref, lnb_ref,
                  y_ref, h_ref, conv_ref, sr_ref, si_ref,
                  zfull, a_buf, b_buf, s5buf):
    r, d = x_ref.shape
    tail = (conv_w - 1) * nb
    s5w = s5buf.shape[1] // 2
    s5k = wb_ref.shape[1]

    @pl.when(pl.program_id(0) == 0)
    def _():
        h_ref[...] = h0_ref[...]
        zfull[0:tail, :] = conv0_ref[...]
        sr_ref[...] = sr0_ref[...]
        si_ref[...] = si0_ref[...]

    x3 = x_ref[...].reshape(tb, nb, d)
    ada = lambda k: ada_ref[:, k * d:(k + 1) * d][None]
    u = x3 * (1.0 + ada(1)) + ada(0)
    ub = u.reshape(r, d).astype(BF16)

    def proj(k):
        return _dot(ub, w_in_ref[:, k * d:(k + 1) * d]) + b_in_ref[:, k * d:(k + 1) * d]

    zfull[tail:, :] = proj(0)
    v = b_conv_ref[...]
    for k in range(conv_w):
        v = v + w_conv_ref[k:k + 1, :] * zfull[k * nb:k * nb + r, :]
    new_tail = zfull[r:r + tail, :]
    zfull[0:tail, :] = new_tail
    conv_ref[...] = new_tail

    vb = v.astype(BF16)
    gw = d // n_gate_tiles

    def gate(gi):
        cols = [_dot(vb[:, j * gw:(j + 1) * gw], wg_ref[gi, j]) for j in range(n_gate_tiles)]
        return jax.nn.sigmoid(jnp.concatenate(cols, axis=-1) + bg_ref[gi:gi + 1, :])

    log_a = (-LRU_C) * gate(0) * _softplus(-lam_ref[...])
    a = jnp.exp(log_a)
    b_buf[...] = jnp.sqrt(1.0 - a * a) * (gate(1) * v)
    a_buf[...] = a

    def lru_step(row, h):
        h = a_buf[pl.ds(row, nb), :] * h + b_buf[pl.ds(row, nb), :]
        b_buf[pl.ds(row, nb), :] = h
        return h

    h_ref[...] = _scan_slabs(tb, nb, lru_step, h_ref[...])
    br_a = _dot((b_buf[...] * jax.nn.gelu(proj(1))).astype(BF16), w_lo_ref[...])

    z_s5 = proj(2)
    zsb = z_s5.astype(BF16)
    ys = []
    for c in range(n_s5_chunks):
        cols = slice(c * s5w, (c + 1) * s5w)
        s5buf[...] = _dot(zsb[:, c * s5k:(c + 1) * s5k], wb_ref[c])
        ar = jnp.broadcast_to(ar_ref[:, cols], (nb, s5w))
        ai = jnp.broadcast_to(ai_ref[:, cols], (nb, s5w))

        def s5_step(row, carry, ar=ar, ai=ai):
            xr, xi = carry
            rows = pl.ds(row, nb)
            nxr = ar * xr - ai * xi + s5buf[rows, 0:s5w]
            nxi = ar * xi + ai * xr + s5buf[rows, s5w:2 * s5w]
            s5buf[rows, 0:s5w] = nxr
            s5buf[rows, s5w:2 * s5w] = nxi
            return nxr, nxi

        xr, xi = _scan_slabs(tb, nb, s5_step, (sr_ref[:, cols], si_ref[:, cols]))
        sr_ref[:, cols] = xr
        si_ref[:, cols] = xi
        ys.append(_dot(s5buf[...].astype(BF16), wc_ref[c]))
    ys = jnp.concatenate(ys, axis=-1) + d_ref[...] * z_s5
    gl = _dot(jax.nn.gelu(ys).astype(BF16), w_glu_ref[...])
    br_b = gl[:, :d] * jax.nn.sigmoid(gl[:, d:])

    mixed = jax.nn.sigmoid(proj(3)) * br_a + jax.nn.sigmoid(proj(4)) * br_b
    mix = _dot(mixed.astype(BF16), w_out_ref[...]).reshape(tb, nb, d)
    res = alpha * x3 + ada(2) * mix
    y_ref[...] = _layer_norm(res, lng_ref[...], lnb_ref[...]).reshape(r, d)


def _mixer_call(x, ada, h0, conv0, sr0, si0, lw, *, nb, alpha):
    rows, d = x.shape
    t_total = rows // nb
    tb = min(t_total, ROWS_PER_BLOCK // nb)
    r = tb * nb
    conv_w = lw["w_conv"].shape[0]
    n_gate_tiles = lw["wg"].shape[1]
    n_chunks, s5k, s5w2 = lw["wb"].shape
    n_state = sr0.shape[1]
    tail = (conv_w - 1) * nb

    weights = [lw[k] for k in ("w_in", "b_in", "w_conv", "b_conv", "wg", "bg", "lam",
                               "w_lo", "ar", "ai", "wb", "wc", "s5_d", "w_glu", "w_out",
                               "ln1_g", "ln1_b")]
    state_in = [ada, h0, conv0, sr0, si0]
    row_spec = pl.BlockSpec((r, d), lambda i: (i, 0))
    kern = functools.partial(_mixer_kernel, tb, nb, conv_w, n_gate_tiles, n_chunks, alpha)
    return pl.pallas_call(
        kern,
        grid=(t_total // tb,),
        in_specs=[row_spec] + [_resident(a.shape) for a in state_in + weights],
        out_specs=[row_spec, _carried((nb, d)), _carried((tail, d)),
                   _carried((nb, n_state)), _carried((nb, n_state))],
        out_shape=[jax.ShapeDtypeStruct((rows, d), F32),
                   jax.ShapeDtypeStruct((nb, d), F32),
                   jax.ShapeDtypeStruct((tail, d), F32),
                   jax.ShapeDtypeStruct((nb, n_state), F32),
                   jax.ShapeDtypeStruct((nb, n_state), F32)],
        scratch_shapes=[pltpu.VMEM((r + tail, d), F32),
                        pltpu.VMEM((r, d), F32),
                        pltpu.VMEM((r, d), F32),
                        pltpu.VMEM((r, s5w2), F32)],
        compiler_params=pltpu.CompilerParams(
            dimension_semantics=("arbitrary",), vmem_limit_bytes=VMEM_LIMIT),
        name="mixer",
    )(x, *state_in, *weights)


def _ffn_kernel(tb, nb, alpha, x_ref, ada_ref, w_up_ref, w_down_ref, lng_ref, lnb_ref, y_ref):
    r, d = x_ref.shape
    d_ff = w_down_ref.shape[0]
    x3 = x_ref[...].reshape(tb, nb, d)
    ada = lambda k: ada_ref[:, k * d:(k + 1) * d][None]
    u = (x3 * (1.0 + ada(4)) + ada(3)).reshape(r, d).astype(BF16)
    hf = _dot(u, w_up_ref[...])
    act = (jax.nn.silu(hf[:, :d_ff]) * hf[:, d_ff:]).astype(BF16)
    f = _dot(act, w_down_ref[...]).reshape(tb, nb, d)
    res = alpha * x3 + ada(5) * f
    y_ref[...] = _layer_norm(res, lng_ref[...], lnb_ref[...]).reshape(r, d)


def _ffn_call(x, ada, lw, *, nb, alpha):
    rows, d = x.shape
    t_total = rows // nb
    tb = min(t_total, ROWS_PER_BLOCK // nb)
    r = tb * nb
    weights = [lw[k] for k in ("w_up", "w_down", "ln2_g", "ln2_b")]
    row_spec = pl.BlockSpec((r, d), lambda i: (i, 0))
    return pl.pallas_call(
        functools.partial(_ffn_kernel, tb, nb, alpha),
        grid=(t_total // tb,),
        in_specs=[row_spec] + [_resident(a.shape) for a in [ada] + weights],
        out_specs=row_spec,
        out_shape=jax.ShapeDtypeStruct((rows, d), F32),
        compiler_params=pltpu.CompilerParams(
            dimension_semantics=("arbitrary",), vmem_limit_bytes=VMEM_LIMIT),
        name="ffn",
    )(x, ada, *weights)


def _block_diag(blocks):
    n, a, b = blocks.shape
    eye = jnp.eye(n, dtype=blocks.dtype)
    return (blocks[:, :, None, :] * eye[:, None, :, None]).reshape(n * a, n * b)


def _layer_weights(l, p, abr, abi, bbr, bbi):
    d = p["w_in"].shape[1]
    row = lambda a: a.reshape(1, -1)
    wgl = p["w_lru_gates"][l]
    heads, bh = wgl.shape[1], wgl.shape[2]
    hpt = max(1, MXU_DIM // bh)
    wg = jnp.stack([
        jnp.stack([_block_diag(wgl[g, j * hpt:(j + 1) * hpt]) for j in range(heads // hpt)])
        for g in range(2)]).astype(BF16)
    g_n, p_n, q_n = bbr.shape[1:]
    gpc = MXU_DIM // q_n
    n_chunks = g_n // gpc
    wb, wc = [], []
    for c in range(n_chunks):
        gs = slice(c * gpc, (c + 1) * gpc)
        wb.append(jnp.concatenate(
            [_block_diag(jnp.swapaxes(bbr[l, gs], 1, 2)),
             _block_diag(jnp.swapaxes(bbi[l, gs], 1, 2))], axis=1))
        wc.append(jnp.concatenate(
            [_block_diag(jnp.swapaxes(p["s5_c_re"][l, gs], 1, 2)),
             _block_diag(-jnp.swapaxes(p["s5_c_im"][l, gs], 1, 2))], axis=0))
    return dict(
        w_in=p["w_in"][l].astype(BF16), b_in=row(p["b_in"][l]),
        w_conv=p["w_conv"][l], b_conv=row(p["b_conv"][l]),
        wg=wg, bg=p["b_lru_gates"][l], lam=row(p["lru_lambda"][l]),
        w_lo=p["w_lru_out"][l].astype(BF16),
        ar=row(abr[l]), ai=row(abi[l]),
        wb=jnp.stack(wb).astype(BF16), wc=jnp.stack(wc).astype(BF16),
        s5_d=row(p["s5_d"][l]), w_glu=p["w_s5_glu"][l].astype(BF16),
        w_out=p["w_out"][l].astype(BF16),
        ln1_g=row(p["ln1_g"][l]), ln1_b=row(p["ln1_b"][l]),
        w_up=p["w_ffn_up"][l].astype(BF16), w_down=p["w_ffn_down"][l].astype(BF16),
        ln2_g=row(p["ln2_g"][l]), ln2_b=row(p["ln2_b"][l]),
    )


def _time_major(x):
    b, t, d = x.shape
    return jnp.swapaxes(x, 0, 1).reshape(t * b, d)


def _batch_major(x, b):
    rows, d = x.shape
    return jnp.swapaxes(x.reshape(rows // b, b, d), 0, 1)


def kernel(x_prompt, x_sample, c_prompt, c_sample, state_lru_h, state_lru_conv, state_s5_re, state_s5_im, w_ada, b_ada, w_in, b_in, w_conv, b_conv, w_lru_gates, b_lru_gates, lru_lambda, w_lru_out, s5_a_re, s5_a_im, s5_log_dt, s5_b_re, s5_b_im, s5_c_re, s5_c_im, s5_d, w_s5_glu, w_out, ln1_g, ln1_b, w_ffn_up, w_ffn_down, ln2_g, ln2_b):
    p = dict(w_in=w_in, b_in=b_in, w_conv=w_conv, b_conv=b_conv, w_lru_gates=w_lru_gates,
             b_lru_gates=b_lru_gates, lru_lambda=lru_lambda, w_lru_out=w_lru_out,
             s5_c_re=s5_c_re, s5_c_im=s5_c_im, s5_d=s5_d, w_s5_glu=w_s5_glu, w_out=w_out,
             ln1_g=ln1_g, ln1_b=ln1_b, w_ffn_up=w_ffn_up, w_ffn_down=w_ffn_down,
             ln2_g=ln2_g, ln2_b=ln2_b)
    depth = w_ada.shape[0]
    bp, _, d = x_prompt.shape
    bs = x_sample.shape[0]
    conv_w = w_conv.shape[1]
    n_state = s5_a_re.shape[1] * s5_a_re.shape[2]
    alpha = float((2 * depth) ** 0.25)

    ada_all = _ada_call(jnp.concatenate([c_prompt, c_sample], axis=0), w_ada, b_ada)
    abr, abi, bbr, bbi = _s5_discretise(s5_a_re, s5_a_im, s5_log_dt, s5_b_re, s5_b_im)

    xp, xs = _time_major(x_prompt), _time_major(x_sample)
    zeros_p = lambda n: jnp.zeros((n, d), F32)
    zeros_state = jnp.zeros((bp, n_state), F32)
    outs = [[] for _ in range(8)]
    for l in range(depth):
        lw = _layer_weights(l, p, abr, abi, bbr, bbi)
        ada_p, ada_s = ada_all[l, :bp], ada_all[l, bp:]
        xp, hp, cvp, srp, sip = _mixer_call(
            xp, ada_p, zeros_p(bp), zeros_p((conv_w - 1) * bp), zeros_state, zeros_state,
            lw, nb=bp, alpha=alpha)
        xp = _ffn_call(xp, ada_p, lw, nb=bp, alpha=alpha)
        xs, hs, cvs, srs, sis = _mixer_call(
            xs, ada_s, state_lru_h[l], _time_major(state_lru_conv[l]),
            state_s5_re[l].reshape(bs, n_state), state_s5_im[l].reshape(bs, n_state),
            lw, nb=bs, alpha=alpha)
        xs = _ffn_call(xs, ada_s, lw, nb=bs, alpha=alpha)
        sshape = s5_a_re.shape[1:]
        for o, val in zip(outs, (hp, hs, _batch_major(cvp, bp), _batch_major(cvs, bs),
                                 srp.reshape(bp, *sshape), srs.reshape(bs, *sshape),
                                 sip.reshape(bp, *sshape), sis.reshape(bs, *sshape))):
            o.append(val)
    return (_batch_major(xp, bp), _batch_major(xs, bs), *(jnp.stack(o) for o in outs))
```

```python
import functools

import jax
import jax.numpy as jnp
from jax import lax
from jax.experimental import pallas as pl
from jax.experimental.pallas import tpu as pltpu

LRU_C = 8.0
LN_EPS = 1e-5
N_ADA = 6
F32 = jnp.float32
BF16 = jnp.bfloat16

SUBLANES = 8
MXU_DIM = 256
ROWS_PER_BLOCK = 512
VMEM_LIMIT = 60 * 1024 * 1024


def _resident(arr, layer, rows=None, row_block=0):
    shape = list(arr.shape[1:])
    if rows is not None:
        shape[0] = rows
    idx = (layer, row_block) + (0,) * (len(shape) - 1)
    return pl.BlockSpec((None, *shape), lambda *_: idx, pipeline_mode=pl.Buffered(1))


def _carried(shape):
    nd = len(shape)
    return pl.BlockSpec(shape, lambda *_: (0,) * nd)


def _layer_norm(x, g, b):
    mu = jnp.mean(x, axis=-1, keepdims=True)
    xc = x - mu
    var = jnp.mean(xc * xc, axis=-1, keepdims=True)
    return xc * lax.rsqrt(var + LN_EPS) * g + b


def _softplus(x):
    return jnp.maximum(x, 0.0) + jnp.log1p(jnp.exp(-jnp.abs(x)))


def _dot(a, b):
    return jnp.dot(a, b, preferred_element_type=F32)


def _ada_kernel(c_ref, w_ref, b_ref, o_ref):
    s = jax.nn.silu(c_ref[...]).astype(BF16)
    o_ref[...] = _dot(s, w_ref[...].astype(BF16)) + b_ref[...]


def _ada_call(c_all, w_ada, b_ada):
    depth, d, n = w_ada.shape
    m = c_all.shape[0]
    tn = d
    return pl.pallas_call(
        _ada_kernel,
        grid=(depth, n // tn),
        in_specs=[
            pl.BlockSpec((m, d), lambda l, j: (0, 0)),
            pl.BlockSpec((None, d, tn), lambda l, j: (l, 0, j)),
            pl.BlockSpec((None, 1, tn), lambda l, j: (l, 0, j)),
        ],
        out_specs=pl.BlockSpec((None, m, tn), lambda l, j: (l, 0, j)),
        out_shape=jax.ShapeDtypeStruct((depth, m, n), F32),
        compiler_params=pltpu.CompilerParams(
            dimension_semantics=("arbitrary", "arbitrary")),
        name="ada",
    )(c_all, w_ada, b_ada.reshape(depth, 1, n))


def _s5_disc_kernel(a_re_ref, a_im_ref, log_dt_ref, b_re_ref, b_im_ref,
                    abr_ref, abi_ref, bbr_ref, bbi_ref):
    a_re = a_re_ref[...]
    a_im = a_im_ref[...]
    dt = jnp.exp(log_dt_ref[...])
    mag = jnp.exp(a_re * dt)
    abr = mag * jnp.cos(a_im * dt)
    abi = mag * jnp.sin(a_im * dt)
    nr, ni = abr - 1.0, abi
    den = a_re * a_re + a_im * a_im
    fr = (nr * a_re + ni * a_im) / den
    fi = (ni * a_re - nr * a_im) / den
    b_re = b_re_ref[...]
    b_im = b_im_ref[...]
    abr_ref[...] = abr
    abi_ref[...] = abi
    bbr_ref[...] = fr * b_re - fi * b_im
    bbi_ref[...] = fr * b_im + fi * b_re


def _s5_discretise(a_re, a_im, log_dt, b_re, b_im):
    depth, g, p, q = b_re.shape
    shape2 = (depth * g, p * q)
    bc = lambda a: jnp.broadcast_to(a[..., None], (depth, g, p, q)).reshape(shape2)
    dt_b = jnp.broadcast_to(log_dt[..., None, None], (depth, g, p, q)).reshape(shape2)
    outs = pl.pallas_call(
        _s5_disc_kernel,
        out_shape=[jax.ShapeDtypeStruct(shape2, F32)] * 4,
        name="s5_discretise",
    )(bc(a_re), bc(a_im), dt_b, b_re.reshape(shape2), b_im.reshape(shape2))
    abr, abi, bbr, bbi = (o.reshape(depth, g, p, q) for o in outs)
    return abr[..., 0], abi[..., 0], bbr, bbi


def _scan_slabs(tb, nb, body, carry):
    if tb <= 8:
        for t in range(tb):
            carry = body(t * nb, carry)
        return carry

    def step(t, c):
        return body(pl.multiple_of(t * nb, nb), c)

    return lax.fori_loop(0, tb, step, carry, unroll=8)


def _mixer_kernel(tb, nb, conv_w, n_gate_tiles, n_s5_chunks, alpha,
                  x_ref, ada_ref, h0_ref, conv0_ref, sr0_ref, si0_ref,
                  w_in_ref, b_in_ref, w_conv_ref, b_conv_ref, wg_ref, bg_ref,
                  lam_ref, w_lo_ref, ar_ref, ai_ref, wb_ref, wc_ref, d_ref,
                  w_glu_ref, w_out_ref, lng_ref, lnb_ref,
                  y_ref, h_ref, conv_ref, sr_ref, si_ref,
                  zfull, a_buf, b_buf, s5buf):
    r, d = x_ref.shape
    tail = (conv_w - 1) * nb
    s5w = s5buf.shape[1] // 2
    s5k = wb_ref.shape[1]

    @pl.when(pl.program_id(0) == 0)
    def _():
        h_ref[...] = h0_ref[...]
        zfull[0:tail, :] = conv0_ref[...]
        sr_ref[...] = sr0_ref[...]
        si_ref[...] = si0_ref[...]

    x3 = x_ref[...].reshape(tb, nb, d)
    ada = lambda k: ada_ref[:, k * d:(k + 1) * d][None]
    u = x3 * (1.0 + ada(1)) + ada(0)
    ub = u.reshape(r, d).astype(BF16)

    def proj(k):
        return _dot(ub, w_in_ref[:, k * d:(k + 1) * d]) + b_in_ref[:, k * d:(k + 1) * d]

    zfull[tail:, :] = proj(0)
    v = b_conv_ref[...]
    for k in range(conv_w):
        v = v + w_conv_ref[k:k + 1, :] * zfull[k * nb:k * nb + r, :]
    new_tail = zfull[r:r + tail, :]
    zfull[0:tail, :] = new_tail
    conv_ref[...] = new_tail

    vb = v.astype(BF16)
    gw = d // n_gate_tiles

    def gate(gi):
        cols = [_dot(vb[:, j * gw:(j + 1) * gw], wg_ref[gi, j]) for j in range(n_gate_tiles)]
        return jax.nn.sigmoid(jnp.concatenate(cols, axis=-1) + bg_ref[gi:gi + 1, :])

    log_a = (-LRU_C) * gate(0) * _softplus(-lam_ref[...])
    a = jnp.exp(log_a)
    b_buf[...] = jnp.sqrt(1.0 - a * a) * (gate(1) * v)
    a_buf[...] = a

    def lru_step(row, h):
        h = a_buf[pl.ds(row, nb), :] * h + b_buf[pl.ds(row, nb), :]
        b_buf[pl.ds(row, nb), :] = h
        return h

    h_ref[...] = _scan_slabs(tb, nb, lru_step, h_ref[...])
    br_a = _dot((b_buf[...] * jax.nn.gelu(proj(1))).astype(BF16), w_lo_ref[...])

    z_s5 = proj(2)
    zsb = z_s5.astype(BF16)
    ys = []
    for c in range(n_s5_chunks):
        cols = slice(c * s5w, (c + 1) * s5w)
        s5buf[...] = _dot(zsb[:, c * s5k:(c + 1) * s5k], wb_ref[c])
        ar = jnp.broadcast_to(ar_ref[:, cols], (nb, s5w))
        ai = jnp.broadcast_to(ai_ref[:, cols], (nb, s5w))

        def s5_step(row, carry, ar=ar, ai=ai):
            xr, xi = carry
            rows = pl.ds(row, nb)
            nxr = ar * xr - ai * xi + s5buf[rows, 0:s5w]
            nxi = ar * xi + ai * xr + s5buf[rows, s5w:2 * s5w]
            s5buf[rows, 0:s5w] = nxr
            s5buf[rows, s5w:2 * s5w] = nxi
            return nxr, nxi

        xr, xi = _scan_slabs(tb, nb, s5_step, (sr_ref[:, cols], si_ref[:, cols]))
        sr_ref[:, cols] = xr
        si_ref[:, cols] = xi
        ys.append(_dot(s5buf[...].astype(BF16), wc_ref[c]))
    ys = jnp.concatenate(ys, axis=-1) + d_ref[...] * z_s5
    gl = _dot(jax.nn.gelu(ys).astype(BF16), w_glu_ref[...])
    br_b = gl[:, :d] * jax.nn.sigmoid(gl[:, d:])

    mixed = jax.nn.sigmoid(proj(3)) * br_a + jax.nn.sigmoid(proj(4)) * br_b
    mix = _dot(mixed.astype(BF16), w_out_ref[...]).reshape(tb, nb, d)
    res = alpha * x3 + ada(2) * mix
    y_ref[...] = _layer_norm(res, lng_ref[...], lnb_ref[...]).reshape(r, d)


_MIXER_WEIGHTS = ("w_in", "b_in", "w_conv", "b_conv", "wg", "bg", "lam", "w_lo", "ar", "ai",
                  "wb", "wc", "s5_d", "w_glu", "w_out", "ln1_g", "ln1_b")
_FFN_WEIGHTS = ("w_up", "w_down", "ln2_g", "ln2_b")


def _mixer_call(x, ada, ada_row_block, states, state_layer, lw, layer, *, nb, alpha):
    rows, d = x.shape
    t_total = rows // nb
    tb = min(t_total, ROWS_PER_BLOCK // nb)
    r = tb * nb
    conv_w = lw["w_conv"].shape[1]
    n_gate_tiles = lw["wg"].shape[2]
    n_chunks, _, s5w2 = lw["wb"].shape[1:]
    n_state = states[2].shape[2]
    tail = (conv_w - 1) * nb

    weights = [lw[k] for k in _MIXER_WEIGHTS]
    row_spec = pl.BlockSpec((r, d), lambda i: (i, 0))
    kern = functools.partial(_mixer_kernel, tb, nb, conv_w, n_gate_tiles, n_chunks, alpha)
    return pl.pallas_call(
        kern,
        grid=(t_total // tb,),
        in_specs=([row_spec, _resident(ada, layer, rows=nb, row_block=ada_row_block)]
                  + [_resident(a, state_layer) for a in states]
                  + [_resident(a, layer) for a in weights]),
        out_specs=[row_spec, _carried((nb, d)), _carried((tail, d)),
                   _carried((nb, n_state)), _carried((nb, n_state))],
        out_shape=[jax.ShapeDtypeStruct((rows, d), F32),
                   jax.ShapeDtypeStruct((nb, d), F32),
                   jax.ShapeDtypeStruct((tail, d), F32),
                   jax.ShapeDtypeStruct((nb, n_state), F32),
                   jax.ShapeDtypeStruct((nb, n_state), F32)],
        scratch_shapes=[pltpu.VMEM((r + tail, d), F32),
                        pltpu.VMEM((r, d), F32),
                        pltpu.VMEM((r, d), F32),
                        pltpu.VMEM((r, s5w2), F32)],
        compiler_params=pltpu.CompilerParams(
            dimension_semantics=("arbitrary",), vmem_limit_bytes=VMEM_LIMIT),
        name="mixer",
    )(x, ada, *states, *weights)


def _ffn_kernel(tb, nb, alpha, x_ref, ada_ref, w_up_ref, w_down_ref, lng_ref, lnb_ref, y_ref):
    r, d = x_ref.shape
    d_ff = w_down_ref.shape[0]
    x3 = x_ref[...].reshape(tb, nb, d)
    ada = lambda k: ada_ref[:, k * d:(k + 1) * d][None]
    u = (x3 * (1.0 + ada(4)) + ada(3)).reshape(r, d).astype(BF16)
    hf = _dot(u, w_up_ref[...])
    act = (jax.nn.silu(hf[:, :d_ff]) * hf[:, d_ff:]).astype(BF16)
    f = _dot(act, w_down_ref[...]).reshape(tb, nb, d)
    res = alpha * x3 + ada(5) * f
    y_ref[...] = _layer_norm(res, lng_ref[...], lnb_ref[...]).reshape(r, d)


def _ffn_call(x, ada, ada_row_block, lw, layer, *, nb, alpha):
    rows, d = x.shape
    t_total = rows // nb
    tb = min(t_total, ROWS_PER_BLOCK // nb)
    r = tb * nb
    weights = [lw[k] for k in _FFN_WEIGHTS]
    row_spec = pl.BlockSpec((r, d), lambda i: (i, 0))
    return pl.pallas_call(
        functools.partial(_ffn_kernel, tb, nb, alpha),
        grid=(t_total // tb,),
        in_specs=([row_spec, _resident(ada, layer, rows=nb, row_block=ada_row_block)]
                  + [_resident(a, layer) for a in weights]),
        out_specs=row_spec,
        out_shape=jax.ShapeDtypeStruct((rows, d), F32),
        compiler_params=pltpu.CompilerParams(
            dimension_semantics=("arbitrary",), vmem_limit_bytes=VMEM_LIMIT),
        name="ffn",
    )(x, ada, *weights)


def _block_diag(blocks):
    *lead, n, a, b = blocks.shape
    eye = jnp.eye(n, dtype=blocks.dtype)
    out = blocks[..., :, :, None, :] * eye[:, None, :, None]
    return out.reshape(*lead, n * a, n * b)


def _pack_weights(p, abr, abi, bbr, bbi):
    depth = p["w_in"].shape[0]
    row = lambda a: a.reshape(depth, 1, -1)
    wgl = p["w_lru_gates"]
    heads, bh = wgl.shape[2], wgl.shape[3]
    hpt = max(1, MXU_DIM // bh)
    wg = _block_diag(wgl.reshape(depth, 2, heads // hpt, hpt, bh, bh))
    g_n, _, q_n = bbr.shape[1:]
    gpc = MXU_DIM // q_n
    chunked = lambda a: a.reshape(depth, g_n // gpc, gpc, *a.shape[2:])
    packed = lambda a: _block_diag(jnp.swapaxes(chunked(a), -1, -2))
    wb = jnp.concatenate([packed(bbr), packed(bbi)], axis=-1)
    wc = jnp.concatenate([packed(p["s5_c_re"]), packed(-p["s5_c_im"])], axis=-2)
    return dict(
        w_in=p["w_in"].astype(BF16), b_in=row(p["b_in"]),
        w_conv=p["w_conv"], b_conv=row(p["b_conv"]),
        wg=wg.astype(BF16), bg=p["b_lru_gates"], lam=row(p["lru_lambda"]),
        w_lo=p["w_lru_out"].astype(BF16),
        ar=row(abr), ai=row(abi), wb=wb.astype(BF16), wc=wc.astype(BF16),
        s5_d=row(p["s5_d"]), w_glu=p["w_s5_glu"].astype(BF16), w_out=p["w_out"].astype(BF16),
        ln1_g=row(p["ln1_g"]), ln1_b=row(p["ln1_b"]),
        w_up=p["w_ffn_up"].astype(BF16), w_down=p["w_ffn_down"].astype(BF16),
        ln2_g=row(p["ln2_g"]), ln2_b=row(p["ln2_b"]),
    )


def _time_major(x):
    *lead, b, t, d = x.shape
    return jnp.swapaxes(x, -3, -2).reshape(*lead, t * b, d)


def _batch_major(x, b):
    *lead, rows, d = x.shape
    return jnp.swapaxes(x.reshape(*lead, rows // b, b, d), -3, -2)


def kernel(x_prompt, x_sample, c_prompt, c_sample, state_lru_h, state_lru_conv, state_s5_re, state_s5_im, w_ada, b_ada, w_in, b_in, w_conv, b_conv, w_lru_gates, b_lru_gates, lru_lambda, w_lru_out, s5_a_re, s5_a_im, s5_log_dt, s5_b_re, s5_b_im, s5_c_re, s5_c_im, s5_d, w_s5_glu, w_out, ln1_g, ln1_b, w_ffn_up, w_ffn_down, ln2_g, ln2_b):
    p = dict(w_in=w_in, b_in=b_in, w_conv=w_conv, b_conv=b_conv, w_lru_gates=w_lru_gates,
             b_lru_gates=b_lru_gates, lru_lambda=lru_lambda, w_lru_out=w_lru_out,
             s5_c_re=s5_c_re, s5_c_im=s5_c_im, s5_d=s5_d, w_s5_glu=w_s5_glu, w_out=w_out,
             ln1_g=ln1_g, ln1_b=ln1_b, w_ffn_up=w_ffn_up, w_ffn_down=w_ffn_down,
             ln2_g=ln2_g, ln2_b=ln2_b)
    depth = w_ada.shape[0]
    bp, _, d = x_prompt.shape
    bs = x_sample.shape[0]
    conv_w = w_conv.shape[1]
    sshape = s5_a_re.shape[1:]
    n_state = sshape[0] * sshape[1]
    alpha = float((2 * depth) ** 0.25)

    ada_all = _ada_call(jnp.concatenate([c_sample, c_prompt], axis=0), w_ada, b_ada)
    abr, abi, bbr, bbi = _s5_discretise(s5_a_re, s5_a_im, s5_log_dt, s5_b_re, s5_b_im)
    lw = _pack_weights(p, abr, abi, bbr, bbi)

    states_s = (state_lru_h, _time_major(state_lru_conv),
                state_s5_re.reshape(depth, bs, n_state), state_s5_im.reshape(depth, bs, n_state))
    states_p = (jnp.zeros((1, bp, d), F32), jnp.zeros((1, (conv_w - 1) * bp, d), F32),
                jnp.zeros((1, bp, n_state), F32), jnp.zeros((1, bp, n_state), F32))

    xp, xs = _time_major(x_prompt), _time_major(x_sample)
    outs_p, outs_s = [], []
    for l in range(depth):
        xp, *st_p = _mixer_call(xp, ada_all, bs // bp, states_p, 0, lw, l, nb=bp, alpha=alpha)
        xp = _ffn_call(xp, ada_all, bs // bp, lw, l, nb=bp, alpha=alpha)
        xs, *st_s = _mixer_call(xs, ada_all, 0, states_s, l, lw, l, nb=bs, alpha=alpha)
        xs = _ffn_call(xs, ada_all, 0, lw, l, nb=bs, alpha=alpha)
        outs_p.append(st_p)
        outs_s.append(st_s)

    def finish(outs, b):
        h, cv, sr, si = (jnp.stack(o) for o in zip(*outs))
        return (h, _batch_major(cv, b), sr.reshape(depth, b, *sshape), si.reshape(depth, b, *sshape))

    hp, cvp, srp, sip = finish(outs_p, bp)
    hs, cvs, srs, sis = finish(outs_s, bs)
    return (_batch_major(xp, bp), _batch_major(xs, bs), hp, hs, cvp, cvs, srp, srs, sip, sis)
```

```python
import functools

import jax
import jax.numpy as jnp
from jax import lax
from jax.experimental import pallas as pl
from jax.experimental.pallas import tpu as pltpu

LRU_C = 8.0
LN_EPS = 1e-5
N_ADA = 6
F32 = jnp.float32
BF16 = jnp.bfloat16

SUBLANES = 8
MXU_DIM = 256
ROWS_PER_BLOCK = 512
VMEM_LIMIT = 60 * 1024 * 1024


def _resident(arr, layer, rows=None, row_block=0):
    shape = list(arr.shape[1:])
    if rows is not None:
        shape[0] = rows
    idx = (layer, row_block) + (0,) * (len(shape) - 1)
    return pl.BlockSpec((None, *shape), lambda *_: idx, pipeline_mode=pl.Buffered(1))


def _carried(shape):
    nd = len(shape)
    return pl.BlockSpec(shape, lambda *_: (0,) * nd)


def _layer_norm(x, g, b):
    mu = jnp.mean(x, axis=-1, keepdims=True)
    xc = x - mu
    var = jnp.mean(xc * xc, axis=-1, keepdims=True)
    return xc * lax.rsqrt(var + LN_EPS) * g + b


def _softplus(x):
    return jnp.maximum(x, 0.0) + jnp.log1p(jnp.exp(-jnp.abs(x)))


def _dot(a, b):
    return jnp.dot(a, b, preferred_element_type=F32)


def _ada_kernel(c_ref, w_ref, b_ref, o_ref):
    s = jax.nn.silu(c_ref[...]).astype(BF16)
    o_ref[...] = _dot(s, w_ref[...].astype(BF16)) + b_ref[...]


def _ada_call(c_all, w_ada, b_ada):
    depth, d, n = w_ada.shape
    m = c_all.shape[0]
    tn = d
    return pl.pallas_call(
        _ada_kernel,
        grid=(depth, n // tn),
        in_specs=[
            pl.BlockSpec((m, d), lambda l, j: (0, 0)),
            pl.BlockSpec((None, d, tn), lambda l, j: (l, 0, j)),
            pl.BlockSpec((None, 1, tn), lambda l, j: (l, 0, j)),
        ],
        out_specs=pl.BlockSpec((None, m, tn), lambda l, j: (l, 0, j)),
        out_shape=jax.ShapeDtypeStruct((depth, m, n), F32),
        compiler_params=pltpu.CompilerParams(
            dimension_semantics=("arbitrary", "arbitrary")),
        name="ada",
    )(c_all, w_ada, b_ada.reshape(depth, 1, n))


def _s5_disc_kernel(a_re_ref, a_im_ref, log_dt_ref, b_re_ref, b_im_ref,
                    abr_ref, abi_ref, bbr_ref, bbi_ref):
    a_re = a_re_ref[...]
    a_im = a_im_ref[...]
    dt = jnp.exp(log_dt_ref[...])
    mag = jnp.exp(a_re * dt)
    abr = mag * jnp.cos(a_im * dt)
    abi = mag * jnp.sin(a_im * dt)
    nr, ni = abr - 1.0, abi
    den = a_re * a_re + a_im * a_im
    fr = (nr * a_re + ni * a_im) / den
    fi = (ni * a_re - nr * a_im) / den
    b_re = b_re_ref[...]
    b_im = b_im_ref[...]
    abr_ref[...] = abr
    abi_ref[...] = abi
    bbr_ref[...] = fr * b_re - fi * b_im
    bbi_ref[...] = fr * b_im + fi * b_re


def _s5_discretise(a_re, a_im, log_dt, b_re, b_im):
    depth, g, p, q = b_re.shape
    shape2 = (depth * g, p * q)
    bc = lambda a: jnp.broadcast_to(a[..., None], (depth, g, p, q)).reshape(shape2)
    dt_b = jnp.broadcast_to(log_dt[..., None, None], (depth, g, p, q)).reshape(shape2)
    outs = pl.pallas_call(
        _s5_disc_kernel,
        out_shape=[jax.ShapeDtypeStruct(shape2, F32)] * 4,
        name="s5_discretise",
    )(bc(a_re), bc(a_im), dt_b, b_re.reshape(shape2), b_im.reshape(shape2))
    abr, abi, bbr, bbi = (o.reshape(depth, g, p, q) for o in outs)
    return abr[..., 0], abi[..., 0], bbr, bbi


def _scan_slabs(tb, nb, body, carry):
    if tb <= 64:
        for t in range(tb):
            carry = body(t * nb, carry)
        return carry

    def step(t, c):
        return body(pl.multiple_of(t * nb, nb), c)

    return lax.fori_loop(0, tb, step, carry, unroll=8)


def _mixer_kernel(tb, nb, conv_w, n_gate_tiles, n_s5_chunks, alpha,
                  x_ref, ada_ref, h0_ref, conv0_ref, sr0_ref, si0_ref,
                  w_in_ref, b_in_ref, w_conv_ref, b_conv_ref, wg_ref, bg_ref,
                  lam_ref, w_lo_ref, ar_ref, ai_ref, wb_ref, wc_ref, d_ref,
                  w_glu_ref, w_out_ref, lng_ref, lnb_ref,
                  y_ref, h_ref, conv_ref, sr_ref, si_ref,
                  zfull, a_buf, b_buf, s5buf):
    r, d = x_ref.shape
    tail = (conv_w - 1) * nb
    s5w = s5buf.shape[1] // 2
    s5k = wb_ref.shape[1]

    @pl.when(pl.program_id(0) == 0)
    def _():
        h_ref[...] = h0_ref[...]
        zfull[0:tail, :] = conv0_ref[...]
        sr_ref[...] = sr0_ref[...]
        si_ref[...] = si0_ref[...]

    x3 = x_ref[...].reshape(tb, nb, d)
    ada = lambda k: ada_ref[:, k * d:(k + 1) * d][None]
    u = x3 * (1.0 + ada(1)) + ada(0)
    ub = u.reshape(r, d).astype(BF16)

    def proj(k):
        return _dot(ub, w_in_ref[:, k * d:(k + 1) * d]) + b_in_ref[:, k * d:(k + 1) * d]

    zfull[tail:, :] = proj(0)
    v = b_conv_ref[...]
    for k in range(conv_w):
        v = v + w_conv_ref[k:k + 1, :] * zfull[k * nb:k * nb + r, :]
    new_tail = zfull[r:r + tail, :]
    zfull[0:tail, :] = new_tail
    conv_ref[...] = new_tail

    vb = v.astype(BF16)
    gw = d // n_gate_tiles

    def gate(gi):
        cols = [_dot(vb[:, j * gw:(j + 1) * gw], wg_ref[gi, j]) for j in range(n_gate_tiles)]
        return jax.nn.sigmoid(jnp.concatenate(cols, axis=-1) + bg_ref[gi:gi + 1, :])

    log_a = (-LRU_C) * gate(0) * _softplus(-lam_ref[...])
    a = jnp.exp(log_a)
    b_buf[...] = jnp.sqrt(1.0 - a * a) * (gate(1) * v)
    a_buf[...] = a

    def lru_step(row, h):
        h = a_buf[pl.ds(row, nb), :] * h + b_buf[pl.ds(row, nb), :]
        b_buf[pl.ds(row, nb), :] = h
        return h

    h_ref[...] = _scan_slabs(tb, nb, lru_step, h_ref[...])
    br_a = _dot((b_buf[...] * jax.nn.gelu(proj(1))).astype(BF16), w_lo_ref[...])

    z_s5 = proj(2)
    zsb = z_s5.astype(BF16)
    ys = []
    for c in range(n_s5_chunks):
        cols = slice(c * s5w, (c + 1) * s5w)
        s5buf[...] = _dot(zsb[:, c * s5k:(c + 1) * s5k], wb_ref[c])
        ar = jnp.broadcast_to(ar_ref[:, cols], (nb, s5w))
        ai = jnp.broadcast_to(ai_ref[:, cols], (nb, s5w))

        def s5_step(row, carry, ar=ar, ai=ai):
            xr, xi = carry
            rows = pl.ds(row, nb)
            nxr = ar * xr - ai * xi + s5buf[rows, 0:s5w]
            nxi = ar * xi + ai * xr + s5buf[rows, s5w:2 * s5w]
            s5buf[rows, 0:s5w] = nxr
            s5buf[rows, s5w:2 * s5w] = nxi
            return nxr, nxi

        xr, xi = _scan_slabs(tb, nb, s5_step, (sr_ref[:, cols], si_ref[:, cols]))
        sr_ref[:, cols] = xr
        si_ref[:, cols] = xi
        ys.append(_dot(s5buf[...].astype(BF16), wc_ref[c]))
    ys = jnp.concatenate(ys, axis=-1) + d_ref[...] * z_s5
    gl = _dot(jax.nn.gelu(ys).astype(BF16), w_glu_ref[...])
    br_b = gl[:, :d] * jax.nn.sigmoid(gl[:, d:])

    mixed = jax.nn.sigmoid(proj(3)) * br_a + jax.nn.sigmoid(proj(4)) * br_b
    mix = _dot(mixed.astype(BF16), w_out_ref[...]).reshape(tb, nb, d)
    res = alpha * x3 + ada(2) * mix
    y_ref[...] = _layer_norm(res, lng_ref[...], lnb_ref[...]).reshape(r, d)


_MIXER_WEIGHTS = ("w_in", "b_in", "w_conv", "b_conv", "wg", "bg", "lam", "w_lo", "ar", "ai",
                  "wb", "wc", "s5_d", "w_glu", "w_out", "ln1_g", "ln1_b")
_FFN_WEIGHTS = ("w_up", "w_down", "ln2_g", "ln2_b")


def _mixer_call(x, ada, ada_row_block, states, state_layer, lw, layer, *, nb, alpha):
    rows, d = x.shape
    t_total = rows // nb
    tb = min(t_total, ROWS_PER_BLOCK // nb)
    r = tb * nb
    conv_w = lw["w_conv"].shape[1]
    n_gate_tiles = lw["wg"].shape[2]
    n_chunks, _, s5w2 = lw["wb"].shape[1:]
    n_state = states[2].shape[2]
    tail = (conv_w - 1) * nb

    weights = [lw[k] for k in _MIXER_WEIGHTS]
    row_spec = pl.BlockSpec((r, d), lambda i: (i, 0))
    kern = functools.partial(_mixer_kernel, tb, nb, conv_w, n_gate_tiles, n_chunks, alpha)
    return pl.pallas_call(
        kern,
        grid=(t_total // tb,),
        in_specs=([row_spec, _resident(ada, layer, rows=nb, row_block=ada_row_block)]
                  + [_resident(a, state_layer) for a in states]
                  + [_resident(a, layer) for a in weights]),
        out_specs=[row_spec, _carried((nb, d)), _carried((tail, d)),
                   _carried((nb, n_state)), _carried((nb, n_state))],
        out_shape=[jax.ShapeDtypeStruct((rows, d), F32),
                   jax.ShapeDtypeStruct((nb, d), F32),
                   jax.ShapeDtypeStruct((tail, d), F32),
                   jax.ShapeDtypeStruct((nb, n_state), F32),
                   jax.ShapeDtypeStruct((nb, n_state), F32)],
        scratch_shapes=[pltpu.VMEM((r + tail, d), F32),
                        pltpu.VMEM((r, d), F32),
                        pltpu.VMEM((r, d), F32),
                        pltpu.VMEM((r, s5w2), F32)],
        compiler_params=pltpu.CompilerParams(
            dimension_semantics=("arbitrary",), vmem_limit_bytes=VMEM_LIMIT),
        name="mixer",
    )(x, ada, *states, *weights)


def _ffn_kernel(tb, nb, alpha, x_ref, ada_ref, w_up_ref, w_down_ref, lng_ref, lnb_ref, y_ref):
    r, d = x_ref.shape
    d_ff = w_down_ref.shape[0]
    x3 = x_ref[...].reshape(tb, nb, d)
    ada = lambda k: ada_ref[:, k * d:(k + 1) * d][None]
    u = (x3 * (1.0 + ada(4)) + ada(3)).reshape(r, d).astype(BF16)
    hf = _dot(u, w_up_ref[...])
    act = (jax.nn.silu(hf[:, :d_ff]) * hf[:, d_ff:]).astype(BF16)
    f = _dot(act, w_down_ref[...]).reshape(tb, nb, d)
    res = alpha * x3 + ada(5) * f
    y_ref[...] = _layer_norm(res, lng_ref[...], lnb_ref[...]).reshape(r, d)


def _ffn_call(x, ada, ada_row_block, lw, layer, *, nb, alpha):
    rows, d = x.shape
    t_total = rows // nb
    tb = min(t_total, ROWS_PER_BLOCK // nb)
    r = tb * nb
    weights = [lw[k] for k in _FFN_WEIGHTS]
    row_spec = pl.BlockSpec((r, d), lambda i: (i, 0))
    return pl.pallas_call(
        functools.partial(_ffn_kernel, tb, nb, alpha),
        grid=(t_total // tb,),
        in_specs=([row_spec, _resident(ada, layer, rows=nb, row_block=ada_row_block)]
                  + [_resident(a, layer) for a in weights]),
        out_specs=row_spec,
        out_shape=jax.ShapeDtypeStruct((rows, d), F32),
        compiler_params=pltpu.CompilerParams(
            dimension_semantics=("arbitrary",), vmem_limit_bytes=VMEM_LIMIT),
        name="ffn",
    )(x, ada, *weights)


def _block_diag(x, n):
    rows, b = x.shape[-2:]
    tiled = jnp.tile(x, (1,) * (x.ndim - 1) + (n,))
    rb = lax.broadcasted_iota(jnp.int32, (rows, n * b), 0) // (rows // n)
    cb = lax.broadcasted_iota(jnp.int32, (rows, n * b), 1) // b
    return jnp.where(rb == cb, tiled, 0.0)


def _pack_weights(p, abr, abi, bbr, bbi):
    depth = p["w_in"].shape[0]
    row = lambda a: a.reshape(depth, 1, -1)
    wgl = p["w_lru_gates"]
    heads, bh = wgl.shape[2], wgl.shape[3]
    hpt = max(1, MXU_DIM // bh)
    wg = _block_diag(wgl.reshape(depth, 2, heads // hpt, hpt * bh, bh), hpt)
    g_n, _, q_n = bbr.shape[1:]
    gpc = MXU_DIM // q_n

    def packed(a):
        m, k = a.shape[2:]
        at = jnp.swapaxes(a, -1, -2).reshape(depth, g_n // gpc, gpc * k, m)
        return _block_diag(at, gpc)

    wb = jnp.concatenate([packed(bbr), packed(bbi)], axis=-1)
    wc = jnp.concatenate([packed(p["s5_c_re"]), packed(-p["s5_c_im"])], axis=-2)
    return dict(
        w_in=p["w_in"].astype(BF16), b_in=row(p["b_in"]),
        w_conv=p["w_conv"], b_conv=row(p["b_conv"]),
        wg=wg.astype(BF16), bg=p["b_lru_gates"], lam=row(p["lru_lambda"]),
        w_lo=p["w_lru_out"].astype(BF16),
        ar=row(abr), ai=row(abi), wb=wb.astype(BF16), wc=wc.astype(BF16),
        s5_d=row(p["s5_d"]), w_glu=p["w_s5_glu"].astype(BF16), w_out=p["w_out"].astype(BF16),
        ln1_g=row(p["ln1_g"]), ln1_b=row(p["ln1_b"]),
        w_up=p["w_ffn_up"].astype(BF16), w_down=p["w_ffn_down"].astype(BF16),
        ln2_g=row(p["ln2_g"]), ln2_b=row(p["ln2_b"]),
    )


def _time_major(x):
    *lead, b, t, d = x.shape
    return jnp.swapaxes(x, -3, -2).reshape(*lead, t * b, d)


def _batch_major(x, b):
    *lead, rows, d = x.shape
    return jnp.swapaxes(x.reshape(*lead, rows // b, b, d), -3, -2)


def kernel(x_prompt, x_sample, c_prompt, c_sample, state_lru_h, state_lru_conv, state_s5_re, state_s5_im, w_ada, b_ada, w_in, b_in, w_conv, b_conv, w_lru_gates, b_lru_gates, lru_lambda, w_lru_out, s5_a_re, s5_a_im, s5_log_dt, s5_b_re, s5_b_im, s5_c_re, s5_c_im, s5_d, w_s5_glu, w_out, ln1_g, ln1_b, w_ffn_up, w_ffn_down, ln2_g, ln2_b):
    p = dict(w_in=w_in, b_in=b_in, w_conv=w_conv, b_conv=b_conv, w_lru_gates=w_lru_gates,
             b_lru_gates=b_lru_gates, lru_lambda=lru_lambda, w_lru_out=w_lru_out,
             s5_c_re=s5_c_re, s5_c_im=s5_c_im, s5_d=s5_d, w_s5_glu=w_s5_glu, w_out=w_out,
             ln1_g=ln1_g, ln1_b=ln1_b, w_ffn_up=w_ffn_up, w_ffn_down=w_ffn_down,
             ln2_g=ln2_g, ln2_b=ln2_b)
    depth = w_ada.shape[0]
    bp, _, d = x_prompt.shape
    bs = x_sample.shape[0]
    conv_w = w_conv.shape[1]
    sshape = s5_a_re.shape[1:]
    n_state = sshape[0] * sshape[1]
    alpha = float((2 * depth) ** 0.25)

    ada_all = _ada_call(jnp.concatenate([c_sample, c_prompt], axis=0), w_ada, b_ada)
    abr, abi, bbr, bbi = _s5_discretise(s5_a_re, s5_a_im, s5_log_dt, s5_b_re, s5_b_im)
    lw = _pack_weights(p, abr, abi, bbr, bbi)

    states_s = (state_lru_h, _time_major(state_lru_conv),
                state_s5_re.reshape(depth, bs, n_state), state_s5_im.reshape(depth, bs, n_state))
    states_p = (jnp.zeros((1, bp, d), F32), jnp.zeros((1, (conv_w - 1) * bp, d), F32),
                jnp.zeros((1, bp, n_state), F32), jnp.zeros((1, bp, n_state), F32))

    xp, xs = _time_major(x_prompt), _time_major(x_sample)
    outs_p, outs_s = [], []
    for l in range(depth):
        xp, *st_p = _mixer_call(xp, ada_all, bs // bp, states_p, 0, lw, l, nb=bp, alpha=alpha)
        xp = _ffn_call(xp, ada_all, bs // bp, lw, l, nb=bp, alpha=alpha)
        xs, *st_s = _mixer_call(xs, ada_all, 0, states_s, l, lw, l, nb=bs, alpha=alpha)
        xs = _ffn_call(xs, ada_all, 0, lw, l, nb=bs, alpha=alpha)
        outs_p.append(st_p)
        outs_s.append(st_s)

    def finish(outs, b):
        h, cv, sr, si = (jnp.stack(o) for o in zip(*outs))
        return (h, _batch_major(cv, b), sr.reshape(depth, b, *sshape), si.reshape(depth, b, *sshape))

    hp, cvp, srp, sip = finish(outs_p, bp)
    hs, cvs, srs, sis = finish(outs_s, bs)
    return (_batch_major(xp, bp), _batch_major(xs, bs), hp, hs, cvp, cvs, srp, srs, sip, sis)
```

```python
import functools

import jax
import jax.numpy as jnp
from jax import lax
from jax.experimental import pallas as pl
from jax.experimental.pallas import tpu as pltpu

LRU_C = 8.0
LN_EPS = 1e-5
N_ADA = 6
F32 = jnp.float32
BF16 = jnp.bfloat16

LANES = 128
MXU_DIM = 256
ROWS_PER_BLOCK = 512
VMEM_LIMIT = 60 * 1024 * 1024


def _resident(arr, layer, rows=None, row_block=0):
    shape = list(arr.shape[1:])
    if rows is not None:
        shape[0] = rows
    idx = (layer, row_block) + (0,) * (len(shape) - 1)
    return pl.BlockSpec((None, *shape), lambda *_: idx, pipeline_mode=pl.Buffered(1))


def _carried(shape):
    nd = len(shape)
    return pl.BlockSpec(shape, lambda *_: (0,) * nd)


def _load_time_major(x_ref, relayout_ref):
    nb, tb, d = x_ref.shape
    for j in range(d // LANES):
        for b in range(nb):
            relayout_ref[j, pl.ds(b, tb, stride=nb), :] = x_ref[b, :, j * LANES:(j + 1) * LANES]
    return jnp.concatenate([relayout_ref[j] for j in range(d // LANES)], axis=-1)


def _store_batch_major(y, y_ref, relayout_ref):
    nb, tb, d = y_ref.shape
    for j in range(d // LANES):
        relayout_ref[j] = y[:, j * LANES:(j + 1) * LANES]
    for j in range(d // LANES):
        for b in range(nb):
            y_ref[b, :, j * LANES:(j + 1) * LANES] = relayout_ref[j, pl.ds(b, tb, stride=nb), :]


def _layer_norm(x, g, b):
    mu = jnp.mean(x, axis=-1, keepdims=True)
    xc = x - mu
    var = jnp.mean(xc * xc, axis=-1, keepdims=True)
    return xc * lax.rsqrt(var + LN_EPS) * g + b


def _softplus(x):
    return jnp.maximum(x, 0.0) + jnp.log1p(jnp.exp(-jnp.abs(x)))


def _dot(a, b):
    return jnp.dot(a, b, preferred_element_type=F32)


def _ada_kernel(c_ref, w_ref, b_ref, o_ref):
    s = jax.nn.silu(c_ref[...]).astype(BF16)
    o_ref[...] = _dot(s, w_ref[...].astype(BF16)) + b_ref[...]


def _ada_call(c_all, w_ada, b_ada):
    depth, d, n = w_ada.shape
    m = c_all.shape[0]
    tn = d
    return pl.pallas_call(
        _ada_kernel,
        grid=(depth, n // tn),
        in_specs=[
            pl.BlockSpec((m, d), lambda l, j: (0, 0)),
            pl.BlockSpec((None, d, tn), lambda l, j: (l, 0, j)),
            pl.BlockSpec((None, 1, tn), lambda l, j: (l, 0, j)),
        ],
        out_specs=pl.BlockSpec((None, m, tn), lambda l, j: (l, 0, j)),
        out_shape=jax.ShapeDtypeStruct((depth, m, n), F32),
        compiler_params=pltpu.CompilerParams(
            dimension_semantics=("arbitrary", "arbitrary")),
        name="ada",
    )(c_all, w_ada, b_ada.reshape(depth, 1, n))


def _s5_disc_kernel(a_re_ref, a_im_ref, log_dt_ref, b_re_ref, b_im_ref,
                    abr_ref, abi_ref, bbr_ref, bbi_ref):
    a_re = a_re_ref[...]
    a_im = a_im_ref[...]
    dt = jnp.exp(log_dt_ref[...])
    mag = jnp.exp(a_re * dt)
    abr = mag * jnp.cos(a_im * dt)
    abi = mag * jnp.sin(a_im * dt)
    nr, ni = abr - 1.0, abi
    den = a_re * a_re + a_im * a_im
    fr = (nr * a_re + ni * a_im) / den
    fi = (ni * a_re - nr * a_im) / den
    b_re = b_re_ref[...]
    b_im = b_im_ref[...]
    abr_ref[...] = abr
    abi_ref[...] = abi
    bbr_ref[...] = fr * b_re - fi * b_im
    bbi_ref[...] = fr * b_im + fi * b_re


def _s5_discretise(a_re, a_im, log_dt, b_re, b_im):
    depth, g, p, q = b_re.shape
    shape2 = (depth * g, p * q)
    bc = lambda a: jnp.broadcast_to(a[..., None], (depth, g, p, q)).reshape(shape2)
    dt_b = jnp.broadcast_to(log_dt[..., None, None], (depth, g, p, q)).reshape(shape2)
    outs = pl.pallas_call(
        _s5_disc_kernel,
        out_shape=[jax.ShapeDtypeStruct(shape2, F32)] * 4,
        name="s5_discretise",
    )(bc(a_re), bc(a_im), dt_b, b_re.reshape(shape2), b_im.reshape(shape2))
    abr, abi, bbr, bbi = (o.reshape(depth, g, p, q) for o in outs)
    return abr[..., 0], abi[..., 0], bbr, bbi


def _scan_slabs(tb, nb, body, carry):
    if tb <= 64:
        for t in range(tb):
            carry = body(t * nb, carry)
        return carry

    def step(t, c):
        return body(pl.multiple_of(t * nb, nb), c)

    return lax.fori_loop(0, tb, step, carry, unroll=8)


def _mixer_kernel(tb, nb, conv_w, n_gate_tiles, n_s5_chunks, alpha,
                  x_ref, ada_ref, h0_ref, conv0_ref, sr0_ref, si0_ref,
                  w_in_ref, b_in_ref, w_conv_ref, b_conv_ref, wg_ref, bg_ref,
                  lam_ref, w_lo_ref, ar_ref, ai_ref, wb_ref, wc_ref, d_ref,
                  w_glu_ref, w_out_ref, lng_ref, lnb_ref,
                  y_ref, h_ref, conv_ref, sr_ref, si_ref,
                  zfull, a_buf, b_buf, s5buf, *relayout):
    r, d = y_ref.shape
    tail = (conv_w - 1) * nb
    s5w = s5buf.shape[1] // 2
    s5k = wb_ref.shape[1]

    @pl.when(pl.program_id(0) == 0)
    def _():
        h_ref[...] = h0_ref[...]
        zfull[0:tail, :] = conv0_ref[...]
        sr_ref[...] = sr0_ref[...]
        si_ref[...] = si0_ref[...]

    x = _load_time_major(x_ref, *relayout) if relayout else x_ref[...]
    x3 = x.reshape(tb, nb, d)
    ada = lambda k: ada_ref[:, k * d:(k + 1) * d][None]
    u = x3 * (1.0 + ada(1)) + ada(0)
    ub = u.reshape(r, d).astype(BF16)

    def proj(k):
        return _dot(ub, w_in_ref[:, k * d:(k + 1) * d]) + b_in_ref[:, k * d:(k + 1) * d]

    zfull[tail:, :] = proj(0)
    v = b_conv_ref[...]
    for k in range(conv_w):
        v = v + w_conv_ref[k:k + 1, :] * zfull[k * nb:k * nb + r, :]
    new_tail = zfull[r:r + tail, :]
    zfull[0:tail, :] = new_tail
    conv_ref[...] = new_tail

    vb = v.astype(BF16)
    gw = d // n_gate_tiles

    def gate(gi):
        cols = [_dot(vb[:, j * gw:(j + 1) * gw], wg_ref[gi, j]) for j in range(n_gate_tiles)]
        return jax.nn.sigmoid(jnp.concatenate(cols, axis=-1) + bg_ref[gi:gi + 1, :])

    log_a = (-LRU_C) * gate(0) * _softplus(-lam_ref[...])
    a = jnp.exp(log_a)
    b_buf[...] = jnp.sqrt(1.0 - a * a) * (gate(1) * v)
    a_buf[...] = a

    def lru_step(row, h):
        h = a_buf[pl.ds(row, nb), :] * h + b_buf[pl.ds(row, nb), :]
        b_buf[pl.ds(row, nb), :] = h
        return h

    h_ref[...] = _scan_slabs(tb, nb, lru_step, h_ref[...])
    br_a = _dot((b_buf[...] * jax.nn.gelu(proj(1))).astype(BF16), w_lo_ref[...])

    z_s5 = proj(2)
    zsb = z_s5.astype(BF16)
    ys = []
    for c in range(n_s5_chunks):
        cols = slice(c * s5w, (c + 1) * s5w)
        s5buf[...] = _dot(zsb[:, c * s5k:(c + 1) * s5k], wb_ref[c])
        ar = jnp.broadcast_to(ar_ref[:, cols], (nb, s5w))
        ai = jnp.broadcast_to(ai_ref[:, cols], (nb, s5w))

        def s5_step(row, carry, ar=ar, ai=ai):
            xr, xi = carry
            rows = pl.ds(row, nb)
            nxr = ar * xr - ai * xi + s5buf[rows, 0:s5w]
            nxi = ar * xi + ai * xr + s5buf[rows, s5w:2 * s5w]
            s5buf[rows, 0:s5w] = nxr
            s5buf[rows, s5w:2 * s5w] = nxi
            return nxr, nxi

        xr, xi = _scan_slabs(tb, nb, s5_step, (sr_ref[:, cols], si_ref[:, cols]))
        sr_ref[:, cols] = xr
        si_ref[:, cols] = xi
        ys.append(_dot(s5buf[...].astype(BF16), wc_ref[c]))
    ys = jnp.concatenate(ys, axis=-1) + d_ref[...] * z_s5
    gl = _dot(jax.nn.gelu(ys).astype(BF16), w_glu_ref[...])
    br_b = gl[:, :d] * jax.nn.sigmoid(gl[:, d:])

    mixed = jax.nn.sigmoid(proj(3)) * br_a + jax.nn.sigmoid(proj(4)) * br_b
    mix = _dot(mixed.astype(BF16), w_out_ref[...]).reshape(tb, nb, d)
    res = alpha * x3 + ada(2) * mix
    y_ref[...] = _layer_norm(res, lng_ref[...], lnb_ref[...]).reshape(r, d)


_MIXER_WEIGHTS = ("w_in", "b_in", "w_conv", "b_conv", "wg", "bg", "lam", "w_lo", "ar", "ai",
                  "wb", "wc", "s5_d", "w_glu", "w_out", "ln1_g", "ln1_b")
_FFN_WEIGHTS = ("w_up", "w_down", "ln2_g", "ln2_b")


def _mixer_call(x, ada, ada_row_block, states, state_layer, lw, layer, *, nb, alpha):
    batch_major = x.ndim == 3
    d = x.shape[-1]
    t_total = x.shape[1] if batch_major else x.shape[0] // nb
    rows = t_total * nb
    tb = min(t_total, ROWS_PER_BLOCK // nb)
    r = tb * nb
    conv_w = lw["w_conv"].shape[1]
    n_gate_tiles = lw["wg"].shape[2]
    n_chunks, _, s5w2 = lw["wb"].shape[1:]
    n_state = states[2].shape[2]
    tail = (conv_w - 1) * nb

    weights = [lw[k] for k in _MIXER_WEIGHTS]
    row_spec = pl.BlockSpec((r, d), lambda i: (i, 0))
    x_spec = pl.BlockSpec((nb, tb, d), lambda i: (0, i, 0)) if batch_major else row_spec
    relayout = [pltpu.VMEM((d // LANES, r, LANES), F32)] if batch_major else []
    kern = functools.partial(_mixer_kernel, tb, nb, conv_w, n_gate_tiles, n_chunks, alpha)
    return pl.pallas_call(
        kern,
        grid=(t_total // tb,),
        in_specs=([x_spec, _resident(ada, layer, rows=nb, row_block=ada_row_block)]
                  + [_resident(a, state_layer) for a in states]
                  + [_resident(a, layer) for a in weights]),
        out_specs=[row_spec, _carried((nb, d)), _carried((tail, d)),
                   _carried((nb, n_state)), _carried((nb, n_state))],
        out_shape=[jax.ShapeDtypeStruct((rows, d), F32),
                   jax.ShapeDtypeStruct((nb, d), F32),
                   jax.ShapeDtypeStruct((tail, d), F32),
                   jax.ShapeDtypeStruct((nb, n_state), F32),
                   jax.ShapeDtypeStruct((nb, n_state), F32)],
        scratch_shapes=[pltpu.VMEM((r + tail, d), F32),
                        pltpu.VMEM((r, d), F32),
                        pltpu.VMEM((r, d), F32),
                        pltpu.VMEM((r, s5w2), F32)] + relayout,
        compiler_params=pltpu.CompilerParams(
            dimension_semantics=("arbitrary",), vmem_limit_bytes=VMEM_LIMIT),
        name="mixer",
    )(x, ada, *states, *weights)


def _ffn_kernel(tb, nb, alpha, x_ref, ada_ref, w_up_ref, w_down_ref, lng_ref, lnb_ref, y_ref,
                *relayout):
    r, d = x_ref.shape
    d_ff = w_down_ref.shape[0]
    x3 = x_ref[...].reshape(tb, nb, d)
    ada = lambda k: ada_ref[:, k * d:(k + 1) * d][None]
    u = (x3 * (1.0 + ada(4)) + ada(3)).reshape(r, d).astype(BF16)
    hf = _dot(u, w_up_ref[...])
    act = (jax.nn.silu(hf[:, :d_ff]) * hf[:, d_ff:]).astype(BF16)
    f = _dot(act, w_down_ref[...]).reshape(tb, nb, d)
    res = alpha * x3 + ada(5) * f
    y = _layer_norm(res, lng_ref[...], lnb_ref[...]).reshape(r, d)
    if relayout:
        _store_batch_major(y, y_ref, *relayout)
    else:
        y_ref[...] = y


def _ffn_call(x, ada, ada_row_block, lw, layer, *, nb, alpha, out_batch_major=False):
    rows, d = x.shape
    t_total = rows // nb
    tb = min(t_total, ROWS_PER_BLOCK // nb)
    r = tb * nb
    weights = [lw[k] for k in _FFN_WEIGHTS]
    row_spec = pl.BlockSpec((r, d), lambda i: (i, 0))
    return pl.pallas_call(
        functools.partial(_ffn_kernel, tb, nb, alpha),
        grid=(t_total // tb,),
        in_specs=([row_spec, _resident(ada, layer, rows=nb, row_block=ada_row_block)]
                  + [_resident(a, layer) for a in weights]),
        out_specs=pl.BlockSpec((nb, tb, d), lambda i: (0, i, 0)) if out_batch_major else row_spec,
        out_shape=jax.ShapeDtypeStruct((nb, t_total, d) if out_batch_major else (rows, d), F32),
        scratch_shapes=[pltpu.VMEM((d // LANES, r, LANES), F32)] if out_batch_major else [],
        compiler_params=pltpu.CompilerParams(
            dimension_semantics=("arbitrary",), vmem_limit_bytes=VMEM_LIMIT),
        name="ffn",
    )(x, ada, *weights)


def _block_diag(x, n):
    rows, b = x.shape[-2:]
    tiled = jnp.tile(x, (1,) * (x.ndim - 1) + (n,))
    rb = lax.broadcasted_iota(jnp.int32, (rows, n * b), 0) // (rows // n)
    cb = lax.broadcasted_iota(jnp.int32, (rows, n * b), 1) // b
    return jnp.where(rb == cb, tiled, 0.0)


def _pack_weights(p, abr, abi, bbr, bbi):
    depth = p["w_in"].shape[0]
    row = lambda a: a.reshape(depth, 1, -1)
    wgl = p["w_lru_gates"]
    heads, bh = wgl.shape[2], wgl.shape[3]
    hpt = max(1, MXU_DIM // bh)
    wg = _block_diag(wgl.reshape(depth, 2, heads // hpt, hpt * bh, bh), hpt)
    g_n, _, q_n = bbr.shape[1:]
    gpc = MXU_DIM // q_n

    def packed(a):
        m, k = a.shape[2:]
        at = jnp.swapaxes(a, -1, -2).reshape(depth, g_n // gpc, gpc * k, m)
        return _block_diag(at, gpc)

    wb = jnp.concatenate([packed(bbr), packed(bbi)], axis=-1)
    wc = jnp.concatenate([packed(p["s5_c_re"]), packed(-p["s5_c_im"])], axis=-2)
    return dict(
        w_in=p["w_in"].astype(BF16), b_in=row(p["b_in"]),
        w_conv=p["w_conv"], b_conv=row(p["b_conv"]),
        wg=wg.astype(BF16), bg=p["b_lru_gates"], lam=row(p["lru_lambda"]),
        w_lo=p["w_lru_out"].astype(BF16),
        ar=row(abr), ai=row(abi), wb=wb.astype(BF16), wc=wc.astype(BF16),
        s5_d=row(p["s5_d"]), w_glu=p["w_s5_glu"].astype(BF16), w_out=p["w_out"].astype(BF16),
        ln1_g=row(p["ln1_g"]), ln1_b=row(p["ln1_b"]),
        w_up=p["w_ffn_up"].astype(BF16), w_down=p["w_ffn_down"].astype(BF16),
        ln2_g=row(p["ln2_g"]), ln2_b=row(p["ln2_b"]),
    )


def _time_major(x):
    *lead, b, t, d = x.shape
    return jnp.swapaxes(x, -3, -2).reshape(*lead, t * b, d)


def _batch_major(x, b):
    *lead, rows, d = x.shape
    return jnp.swapaxes(x.reshape(*lead, rows // b, b, d), -3, -2)


def kernel(x_prompt, x_sample, c_prompt, c_sample, state_lru_h, state_lru_conv, state_s5_re, state_s5_im, w_ada, b_ada, w_in, b_in, w_conv, b_conv, w_lru_gates, b_lru_gates, lru_lambda, w_lru_out, s5_a_re, s5_a_im, s5_log_dt, s5_b_re, s5_b_im, s5_c_re, s5_c_im, s5_d, w_s5_glu, w_out, ln1_g, ln1_b, w_ffn_up, w_ffn_down, ln2_g, ln2_b):
    p = dict(w_in=w_in, b_in=b_in, w_conv=w_conv, b_conv=b_conv, w_lru_gates=w_lru_gates,
             b_lru_gates=b_lru_gates, lru_lambda=lru_lambda, w_lru_out=w_lru_out,
             s5_c_re=s5_c_re, s5_c_im=s5_c_im, s5_d=s5_d, w_s5_glu=w_s5_glu, w_out=w_out,
             ln1_g=ln1_g, ln1_b=ln1_b, w_ffn_up=w_ffn_up, w_ffn_down=w_ffn_down,
             ln2_g=ln2_g, ln2_b=ln2_b)
    depth = w_ada.shape[0]
    bp, _, d = x_prompt.shape
    bs = x_sample.shape[0]
    conv_w = w_conv.shape[1]
    sshape = s5_a_re.shape[1:]
    n_state = sshape[0] * sshape[1]
    alpha = float((2 * depth) ** 0.25)

    ada_all = _ada_call(jnp.concatenate([c_sample, c_prompt], axis=0), w_ada, b_ada)
    abr, abi, bbr, bbi = _s5_discretise(s5_a_re, s5_a_im, s5_log_dt, s5_b_re, s5_b_im)
    lw = _pack_weights(p, abr, abi, bbr, bbi)

    states_s = (state_lru_h, _time_major(state_lru_conv),
                state_s5_re.reshape(depth, bs, n_state), state_s5_im.reshape(depth, bs, n_state))
    states_p = (jnp.zeros((1, bp, d), F32), jnp.zeros((1, (conv_w - 1) * bp, d), F32),
                jnp.zeros((1, bp, n_state), F32), jnp.zeros((1, bp, n_state), F32))

    xp, xs = x_prompt, _time_major(x_sample)
    outs_p, outs_s = [], []
    for l in range(depth):
        xp, *st_p = _mixer_call(xp, ada_all, bs // bp, states_p, 0, lw, l, nb=bp, alpha=alpha)
        xp = _ffn_call(xp, ada_all, bs // bp, lw, l, nb=bp, alpha=alpha,
                       out_batch_major=(l == depth - 1))
        xs, *st_s = _mixer_call(xs, ada_all, 0, states_s, l, lw, l, nb=bs, alpha=alpha)
        xs = _ffn_call(xs, ada_all, 0, lw, l, nb=bs, alpha=alpha)
        outs_p.append(st_p)
        outs_s.append(st_s)

    def finish(outs, b):
        h, cv, sr, si = (jnp.stack(o) for o in zip(*outs))
        return (h, _batch_major(cv, b), sr.reshape(depth, b, *sshape), si.reshape(depth, b, *sshape))

    hp, cvp, srp, sip = finish(outs_p, bp)
    hs, cvs, srs, sis = finish(outs_s, bs)
    return (xp, _batch_major(xs, bs), hp, hs, cvp, cvs, srp, srs, sip, sis)
```

```python
import functools

import jax
import jax.numpy as jnp
from jax import lax
from jax.experimental import pallas as pl
from jax.experimental.pallas import tpu as pltpu

LRU_C = 8.0
LN_EPS = 1e-5
N_ADA = 6
F32 = jnp.float32
BF16 = jnp.bfloat16

LANES = 128
MXU_DIM = 256
ROWS_PER_BLOCK = 512
VMEM_LIMIT = 60 * 1024 * 1024


def _resident(arr, layer, rows=None, row_block=0):
    shape = list(arr.shape[1:])
    if rows is not None:
        shape[0] = rows
    idx = (layer, row_block) + (0,) * (len(shape) - 1)
    return pl.BlockSpec((None, *shape), lambda *_: idx, pipeline_mode=pl.Buffered(1))


def _carried(shape):
    nd = len(shape)
    return pl.BlockSpec(shape, lambda *_: (0,) * nd)


def _load_time_major(x_ref, relayout_ref):
    nb, tb, d = x_ref.shape
    for j in range(d // LANES):
        for b in range(nb):
            relayout_ref[j, pl.ds(b, tb, stride=nb), :] = x_ref[b, :, j * LANES:(j + 1) * LANES]
    return jnp.concatenate([relayout_ref[j] for j in range(d // LANES)], axis=-1)


def _store_batch_major(y, y_ref, relayout_ref):
    nb, tb, d = y_ref.shape
    for j in range(d // LANES):
        relayout_ref[j] = y[:, j * LANES:(j + 1) * LANES]
    for j in range(d // LANES):
        for b in range(nb):
            y_ref[b, :, j * LANES:(j + 1) * LANES] = relayout_ref[j, pl.ds(b, tb, stride=nb), :]


def _layer_norm(x, g, b):
    mu = jnp.mean(x, axis=-1, keepdims=True)
    xc = x - mu
    var = jnp.mean(xc * xc, axis=-1, keepdims=True)
    return xc * lax.rsqrt(var + LN_EPS) * g + b


def _softplus(x):
    return jnp.maximum(x, 0.0) + jnp.log1p(jnp.exp(-jnp.abs(x)))


def _dot(a, b):
    return jnp.dot(a, b, preferred_element_type=F32)


def _ada_kernel(c_ref, w_ref, b_ref, o_ref):
    s = jax.nn.silu(c_ref[...]).astype(BF16)
    o_ref[...] = _dot(s, w_ref[...].astype(BF16)) + b_ref[...]


def _ada_call(c_all, w_ada, b_ada):
    depth, d, n = w_ada.shape
    m = c_all.shape[0]
    tn = 2 * d if n % (2 * d) == 0 else d
    return pl.pallas_call(
        _ada_kernel,
        grid=(depth, n // tn),
        in_specs=[
            pl.BlockSpec((m, d), lambda l, j: (0, 0)),
            pl.BlockSpec((None, d, tn), lambda l, j: (l, 0, j)),
            pl.BlockSpec((None, 1, tn), lambda l, j: (l, 0, j)),
        ],
        out_specs=pl.BlockSpec((None, m, tn), lambda l, j: (l, 0, j)),
        out_shape=jax.ShapeDtypeStruct((depth, m, n), F32),
        compiler_params=pltpu.CompilerParams(
            dimension_semantics=("arbitrary", "arbitrary")),
        name="ada",
    )(c_all, w_ada, b_ada.reshape(depth, 1, n))


def _s5_disc_kernel(a_re_ref, a_im_ref, log_dt_ref, b_re_ref, b_im_ref,
                    abr_ref, abi_ref, bbr_ref, bbi_ref):
    a_re = a_re_ref[...]
    a_im = a_im_ref[...]
    dt = jnp.exp(log_dt_ref[...])
    mag = jnp.exp(a_re * dt)
    abr = mag * jnp.cos(a_im * dt)
    abi = mag * jnp.sin(a_im * dt)
    nr, ni = abr - 1.0, abi
    den = a_re * a_re + a_im * a_im
    fr = (nr * a_re + ni * a_im) / den
    fi = (ni * a_re - nr * a_im) / den
    b_re = b_re_ref[...]
    b_im = b_im_ref[...]
    abr_ref[...] = abr
    abi_ref[...] = abi
    bbr_ref[...] = fr * b_re - fi * b_im
    bbi_ref[...] = fr * b_im + fi * b_re


def _s5_discretise(a_re, a_im, log_dt, b_re, b_im):
    depth, g, p, q = b_re.shape
    shape2 = (depth * g, p * q)
    bc = lambda a: jnp.broadcast_to(a[..., None], (depth, g, p, q)).reshape(shape2)
    dt_b = jnp.broadcast_to(log_dt[..., None, None], (depth, g, p, q)).reshape(shape2)
    outs = pl.pallas_call(
        _s5_disc_kernel,
        out_shape=[jax.ShapeDtypeStruct(shape2, F32)] * 4,
        name="s5_discretise",
    )(bc(a_re), bc(a_im), dt_b, b_re.reshape(shape2), b_im.reshape(shape2))
    abr, abi, bbr, bbi = (o.reshape(depth, g, p, q) for o in outs)
    return abr[..., 0], abi[..., 0], bbr, bbi


def _scan_slabs(tb, nb, body, carry):
    if tb <= 64:
        for t in range(tb):
            carry = body(t * nb, carry)
        return carry

    def step(t, c):
        return body(pl.multiple_of(t * nb, nb), c)

    return lax.fori_loop(0, tb, step, carry, unroll=8)


def _mixer_kernel(tb, nb, conv_w, n_gate_tiles, n_s5_chunks, alpha,
                  x_ref, ada_ref, h0_ref, conv0_ref, sr0_ref, si0_ref,
                  w_in_ref, b_in_ref, w_conv_ref, b_conv_ref, wg_ref, bg_ref,
                  lam_ref, w_lo_ref, ar_ref, ai_ref, wb_ref, wc_ref, d_ref,
                  w_glu_ref, w_out_ref, lng_ref, lnb_ref,
                  y_ref, h_ref, conv_ref, sr_ref, si_ref,
                  zfull, a_buf, b_buf, s5re, s5im, *relayout):
    r, d = y_ref.shape
    tail = (conv_w - 1) * nb
    s5k, s5w = wb_ref.shape[2:]

    @pl.when(pl.program_id(0) == 0)
    def _():
        h_ref[...] = h0_ref[...]
        zfull[0:tail, :] = conv0_ref[...]
        sr_ref[...] = sr0_ref[...]
        si_ref[...] = si0_ref[...]

    x = _load_time_major(x_ref, *relayout) if relayout else x_ref[...]
    x3 = x.reshape(tb, nb, d)
    ada = lambda k: ada_ref[:, k * d:(k + 1) * d][None]
    u = x3 * (1.0 + ada(1)) + ada(0)
    ub = u.reshape(r, d).astype(BF16)

    def proj(k):
        return _dot(ub, w_in_ref[:, k * d:(k + 1) * d]) + b_in_ref[:, k * d:(k + 1) * d]

    zfull[tail:, :] = proj(0)
    v = b_conv_ref[...]
    for k in range(conv_w):
        v = v + w_conv_ref[k:k + 1, :] * zfull[k * nb:k * nb + r, :]
    new_tail = zfull[r:r + tail, :]
    zfull[0:tail, :] = new_tail
    conv_ref[...] = new_tail

    vb = v.astype(BF16)
    gw = d // n_gate_tiles

    def gate(gi):
        cols = [_dot(vb[:, j * gw:(j + 1) * gw], wg_ref[gi, j]) for j in range(n_gate_tiles)]
        return jax.nn.sigmoid(jnp.concatenate(cols, axis=-1) + bg_ref[gi:gi + 1, :])

    log_a = (-LRU_C) * gate(0) * _softplus(-lam_ref[...])
    a = jnp.exp(log_a)
    b_buf[...] = jnp.sqrt(1.0 - a * a) * (gate(1) * v)
    a_buf[...] = a

    def lru_step(row, h):
        h = a_buf[pl.ds(row, nb), :] * h + b_buf[pl.ds(row, nb), :]
        b_buf[pl.ds(row, nb), :] = h
        return h

    h_ref[...] = _scan_slabs(tb, nb, lru_step, h_ref[...])
    br_a = _dot((b_buf[...] * jax.nn.gelu(proj(1))).astype(BF16), w_lo_ref[...])

    z_s5 = proj(2)
    zsb = z_s5.astype(BF16)
    ys = []
    for c in range(n_s5_chunks):
        cols = slice(c * s5w, (c + 1) * s5w)
        zc = zsb[:, c * s5k:(c + 1) * s5k]
        s5re[...] = _dot(zc, wb_ref[0, c])
        s5im[...] = _dot(zc, wb_ref[1, c])
        ar = jnp.broadcast_to(ar_ref[:, cols], (nb, s5w))
        ai = jnp.broadcast_to(ai_ref[:, cols], (nb, s5w))

        def s5_step(row, carry, ar=ar, ai=ai):
            xr, xi = carry
            rows = pl.ds(row, nb)
            nxr = ar * xr - ai * xi + s5re[rows, :]
            nxi = ar * xi + ai * xr + s5im[rows, :]
            s5re[rows, :] = nxr
            s5im[rows, :] = nxi
            return nxr, nxi

        xr, xi = _scan_slabs(tb, nb, s5_step, (sr_ref[:, cols], si_ref[:, cols]))
        sr_ref[:, cols] = xr
        si_ref[:, cols] = xi
        ys.append(_dot(s5re[...].astype(BF16), wc_ref[0, c])
                  + _dot(s5im[...].astype(BF16), wc_ref[1, c]))
    ys = jnp.concatenate(ys, axis=-1) + d_ref[...] * z_s5
    gl = _dot(jax.nn.gelu(ys).astype(BF16), w_glu_ref[...])
    br_b = gl[:, :d] * jax.nn.sigmoid(gl[:, d:])

    mixed = jax.nn.sigmoid(proj(3)) * br_a + jax.nn.sigmoid(proj(4)) * br_b
    mix = _dot(mixed.astype(BF16), w_out_ref[...]).reshape(tb, nb, d)
    res = alpha * x3 + ada(2) * mix
    y_ref[...] = _layer_norm(res, lng_ref[...], lnb_ref[...]).reshape(r, d)


_MIXER_WEIGHTS = ("w_in", "b_in", "w_conv", "b_conv", "wg", "bg", "lam", "w_lo", "ar", "ai",
                  "wb", "wc", "s5_d", "w_glu", "w_out", "ln1_g", "ln1_b")
_FFN_WEIGHTS = ("w_up", "w_down", "ln2_g", "ln2_b")


def _mixer_call(x, ada, ada_row_block, states, state_layer, lw, layer, *, nb, alpha):
    batch_major = x.ndim == 3
    d = x.shape[-1]
    t_total = x.shape[1] if batch_major else x.shape[0] // nb
    rows = t_total * nb
    tb = min(t_total, ROWS_PER_BLOCK // nb)
    r = tb * nb
    conv_w = lw["w_conv"].shape[1]
    n_gate_tiles = lw["wg"].shape[2]
    n_chunks, _, s5w = lw["wb"].shape[2:]
    n_state = states[2].shape[2]
    tail = (conv_w - 1) * nb

    weights = [lw[k] for k in _MIXER_WEIGHTS]
    row_spec = pl.BlockSpec((r, d), lambda i: (i, 0))
    x_spec = pl.BlockSpec((nb, tb, d), lambda i: (0, i, 0)) if batch_major else row_spec
    relayout = [pltpu.VMEM((d // LANES, r, LANES), F32)] if batch_major else []
    kern = functools.partial(_mixer_kernel, tb, nb, conv_w, n_gate_tiles, n_chunks, alpha)
    return pl.pallas_call(
        kern,
        grid=(t_total // tb,),
        in_specs=([x_spec, _resident(ada, layer, rows=nb, row_block=ada_row_block)]
                  + [_resident(a, state_layer) for a in states]
                  + [_resident(a, layer) for a in weights]),
        out_specs=[row_spec, _carried((nb, d)), _carried((tail, d)),
                   _carried((nb, n_state)), _carried((nb, n_state))],
        out_shape=[jax.ShapeDtypeStruct((rows, d), F32),
                   jax.ShapeDtypeStruct((nb, d), F32),
                   jax.ShapeDtypeStruct((tail, d), F32),
                   jax.ShapeDtypeStruct((nb, n_state), F32),
                   jax.ShapeDtypeStruct((nb, n_state), F32)],
        scratch_shapes=[pltpu.VMEM((r + tail, d), F32),
                        pltpu.VMEM((r, d), F32),
                        pltpu.VMEM((r, d), F32),
                        pltpu.VMEM((r, s5w), F32),
                        pltpu.VMEM((r, s5w), F32)] + relayout,
        compiler_params=pltpu.CompilerParams(
            dimension_semantics=("arbitrary",), vmem_limit_bytes=VMEM_LIMIT),
        name="mixer",
    )(x, ada, *states, *weights)


def _ffn_kernel(tb, nb, alpha, x_ref, ada_ref, w_up_ref, w_down_ref, lng_ref, lnb_ref, y_ref,
                *relayout):
    r, d = x_ref.shape
    d_ff = w_down_ref.shape[0]
    x3 = x_ref[...].reshape(tb, nb, d)
    ada = lambda k: ada_ref[:, k * d:(k + 1) * d][None]
    u = (x3 * (1.0 + ada(4)) + ada(3)).reshape(r, d).astype(BF16)
    hf = _dot(u, w_up_ref[...])
    act = (jax.nn.silu(hf[:, :d_ff]) * hf[:, d_ff:]).astype(BF16)
    f = _dot(act, w_down_ref[...]).reshape(tb, nb, d)
    res = alpha * x3 + ada(5) * f
    y = _layer_norm(res, lng_ref[...], lnb_ref[...]).reshape(r, d)
    if relayout:
        _store_batch_major(y, y_ref, *relayout)
    else:
        y_ref[...] = y


def _ffn_call(x, ada, ada_row_block, lw, layer, *, nb, alpha, out_batch_major=False):
    rows, d = x.shape
    t_total = rows // nb
    tb = min(t_total, ROWS_PER_BLOCK // nb)
    r = tb * nb
    weights = [lw[k] for k in _FFN_WEIGHTS]
    row_spec = pl.BlockSpec((r, d), lambda i: (i, 0))
    return pl.pallas_call(
        functools.partial(_ffn_kernel, tb, nb, alpha),
        grid=(t_total // tb,),
        in_specs=([row_spec, _resident(ada, layer, rows=nb, row_block=ada_row_block)]
                  + [_resident(a, layer) for a in weights]),
        out_specs=pl.BlockSpec((nb, tb, d), lambda i: (0, i, 0)) if out_batch_major else row_spec,
        out_shape=jax.ShapeDtypeStruct((nb, t_total, d) if out_batch_major else (rows, d), F32),
        scratch_shapes=[pltpu.VMEM((d // LANES, r, LANES), F32)] if out_batch_major else [],
        compiler_params=pltpu.CompilerParams(
            dimension_semantics=("arbitrary",), vmem_limit_bytes=VMEM_LIMIT),
        name="ffn",
    )(x, ada, *weights)


def _block_diag(x, n):
    rows, b = x.shape[-2:]
    tiled = jnp.tile(x, (1,) * (x.ndim - 1) + (n,))
    rb = lax.broadcasted_iota(jnp.int32, (rows, n * b), 0) // (rows // n)
    cb = lax.broadcasted_iota(jnp.int32, (rows, n * b), 1) // b
    return jnp.where(rb == cb, tiled, 0.0)


def _pack_weights(p, abr, abi, bbr, bbi):
    depth = p["w_in"].shape[0]
    row = lambda a: a.reshape(depth, 1, -1)
    wgl = p["w_lru_gates"]
    heads, bh = wgl.shape[2], wgl.shape[3]
    hpt = max(1, MXU_DIM // bh)
    wg = _block_diag(wgl.reshape(depth, 2, heads // hpt, hpt * bh, bh), hpt)
    g_n, _, q_n = bbr.shape[1:]
    gpc = MXU_DIM // q_n

    def packed(re, im):
        a = jnp.stack([re, im], axis=1)
        m, k = a.shape[3:]
        at = jnp.swapaxes(a, -1, -2).reshape(depth, 2, g_n // gpc, gpc * k, m)
        return _block_diag(at, gpc)

    wb = packed(bbr, bbi)
    wc = packed(p["s5_c_re"], -p["s5_c_im"])
    return dict(
        w_in=p["w_in"].astype(BF16), b_in=row(p["b_in"]),
        w_conv=p["w_conv"], b_conv=row(p["b_conv"]),
        wg=wg.astype(BF16), bg=p["b_lru_gates"], lam=row(p["lru_lambda"]),
        w_lo=p["w_lru_out"].astype(BF16),
        ar=row(abr), ai=row(abi), wb=wb.astype(BF16), wc=wc.astype(BF16),
        s5_d=row(p["s5_d"]), w_glu=p["w_s5_glu"].astype(BF16), w_out=p["w_out"].astype(BF16),
        ln1_g=row(p["ln1_g"]), ln1_b=row(p["ln1_b"]),
        w_up=p["w_ffn_up"].astype(BF16), w_down=p["w_ffn_down"].astype(BF16),
        ln2_g=row(p["ln2_g"]), ln2_b=row(p["ln2_b"]),
    )


def _time_major(x):
    *lead, b, t, d = x.shape
    return jnp.swapaxes(x, -3, -2).reshape(*lead, t * b, d)


def _batch_major(x, b):
    *lead, rows, d = x.shape
    return jnp.swapaxes(x.reshape(*lead, rows // b, b, d), -3, -2)


def kernel(x_prompt, x_sample, c_prompt, c_sample, state_lru_h, state_lru_conv, state_s5_re, state_s5_im, w_ada, b_ada, w_in, b_in, w_conv, b_conv, w_lru_gates, b_lru_gates, lru_lambda, w_lru_out, s5_a_re, s5_a_im, s5_log_dt, s5_b_re, s5_b_im, s5_c_re, s5_c_im, s5_d, w_s5_glu, w_out, ln1_g, ln1_b, w_ffn_up, w_ffn_down, ln2_g, ln2_b):
    p = dict(w_in=w_in, b_in=b_in, w_conv=w_conv, b_conv=b_conv, w_lru_gates=w_lru_gates,
             b_lru_gates=b_lru_gates, lru_lambda=lru_lambda, w_lru_out=w_lru_out,
             s5_c_re=s5_c_re, s5_c_im=s5_c_im, s5_d=s5_d, w_s5_glu=w_s5_glu, w_out=w_out,
             ln1_g=ln1_g, ln1_b=ln1_b, w_ffn_up=w_ffn_up, w_ffn_down=w_ffn_down,
             ln2_g=ln2_g, ln2_b=ln2_b)
    depth = w_ada.shape[0]
    bp, _, d = x_prompt.shape
    bs = x_sample.shape[0]
    conv_w = w_conv.shape[1]
    sshape = s5_a_re.shape[1:]
    n_state = sshape[0] * sshape[1]
    alpha = float((2 * depth) ** 0.25)

    ada_all = _ada_call(jnp.concatenate([c_sample, c_prompt], axis=0), w_ada, b_ada)
    abr, abi, bbr, bbi = _s5_discretise(s5_a_re, s5_a_im, s5_log_dt, s5_b_re, s5_b_im)
    lw = _pack_weights(p, abr, abi, bbr, bbi)

    states_s = (state_lru_h, _time_major(state_lru_conv),
                state_s5_re.reshape(depth, bs, n_state), state_s5_im.reshape(depth, bs, n_state))
    states_p = (jnp.zeros((1, bp, d), F32), jnp.zeros((1, (conv_w - 1) * bp, d), F32),
                jnp.zeros((1, bp, n_state), F32), jnp.zeros((1, bp, n_state), F32))

    xp, xs = x_prompt, _time_major(x_sample)
    outs_p, outs_s = [], []
    for l in range(depth):
        xp, *st_p = _mixer_call(xp, ada_all, bs // bp, states_p, 0, lw, l, nb=bp, alpha=alpha)
        xp = _ffn_call(xp, ada_all, bs // bp, lw, l, nb=bp, alpha=alpha,
                       out_batch_major=(l == depth - 1))
        xs, *st_s = _mixer_call(xs, ada_all, 0, states_s, l, lw, l, nb=bs, alpha=alpha)
        xs = _ffn_call(xs, ada_all, 0, lw, l, nb=bs, alpha=alpha)
        outs_p.append(st_p)
        outs_s.append(st_s)

    def finish(outs, b):
        h, cv, sr, si = (jnp.stack(o) for o in zip(*outs))
        return (h, _batch_major(cv, b), sr.reshape(depth, b, *sshape), si.reshape(depth, b, *sshape))

    hp, cvp, srp, sip = finish(outs_p, bp)
    hs, cvs, srs, sis = finish(outs_s, bs)
    return (xp, _batch_major(xs, bs), hp, hs, cvp, cvs, srp, srs, sip, sis)
```

```python
import functools

import jax
import jax.numpy as jnp
from jax import lax
from jax.experimental import pallas as pl
from jax.experimental.pallas import tpu as pltpu

LRU_C = 8.0
LN_EPS = 1e-5
N_ADA = 6
F32 = jnp.float32
BF16 = jnp.bfloat16

LANES = 128
BF16_ROWS = 16
MXU_DIM = 256
ROWS_PER_BLOCK = 512
VMEM_LIMIT = 60 * 1024 * 1024


def _resident(arr, layer, rows=None, row_block=0):
    if isinstance(arr, list):
        shape = arr[layer].shape
        return pl.BlockSpec(shape, lambda *_: (0,) * len(shape), pipeline_mode=pl.Buffered(1))
    shape = list(arr.shape[1:])
    if rows is not None:
        shape[0] = rows
    idx = (layer, row_block) + (0,) * (len(shape) - 1)
    return pl.BlockSpec((None, *shape), lambda *_: idx, pipeline_mode=pl.Buffered(1))


def _carried(shape):
    nd = len(shape)
    return pl.BlockSpec(shape, lambda *_: (0,) * nd)


def _with_casts(body, n_in, n_out, n_cast):
    def kern(*refs):
        ins, refs = refs[:n_in], refs[n_in:]
        cast_in, refs = refs[:n_cast], refs[n_cast:]
        outs, refs = refs[:n_out], refs[n_out:]
        cast_out, scratch = refs[:n_cast], refs[n_cast:]
        for src, dst in zip(cast_in, cast_out):
            dst[...] = src[...].astype(BF16)
        body(*ins, *outs, *scratch)
    return kern


def _cast_specs(w, layer, n_steps):
    rows, cols = w.shape[1:]
    need = -(-rows // n_steps)
    rb = next(b for b in range(BF16_ROWS, rows + 1, BF16_ROWS) if rows % b == 0 and b >= need)
    last = rows // rb - 1
    return (pl.BlockSpec((None, rb, cols), lambda i: (layer, jnp.minimum(i, last), 0)),
            pl.BlockSpec((rb, cols), lambda i: (jnp.minimum(i, last), 0)),
            jax.ShapeDtypeStruct((rows, cols), BF16))


def _load_time_major(x_ref, relayout_ref):
    nb, tb, d = x_ref.shape
    for j in range(d // LANES):
        for b in range(nb):
            relayout_ref[j, pl.ds(b, tb, stride=nb), :] = x_ref[b, :, j * LANES:(j + 1) * LANES]
    return jnp.concatenate([relayout_ref[j] for j in range(d // LANES)], axis=-1)


def _store_batch_major(y, y_ref, relayout_ref):
    nb, tb, d = y_ref.shape
    for j in range(d // LANES):
        relayout_ref[j] = y[:, j * LANES:(j + 1) * LANES]
    for j in range(d // LANES):
        for b in range(nb):
            y_ref[b, :, j * LANES:(j + 1) * LANES] = relayout_ref[j, pl.ds(b, tb, stride=nb), :]


def _layer_norm(x, g, b):
    mu = jnp.mean(x, axis=-1, keepdims=True)
    xc = x - mu
    var = jnp.mean(xc * xc, axis=-1, keepdims=True)
    return xc * lax.rsqrt(var + LN_EPS) * g + b


def _softplus(x):
    return jnp.maximum(x, 0.0) + jnp.log1p(jnp.exp(-jnp.abs(x)))


def _dot(a, b):
    return jnp.dot(a, b, preferred_element_type=F32)


def _ada_kernel(c_ref, w_ref, b_ref, o_ref):
    s = jax.nn.silu(c_ref[...]).astype(BF16)
    o_ref[...] = _dot(s, w_ref[...].astype(BF16)) + b_ref[...]


def _ada_call(c_all, w_ada, b_ada):
    depth, d, n = w_ada.shape
    m = c_all.shape[0]
    tn = 2 * d if n % (2 * d) == 0 else d
    return pl.pallas_call(
        _ada_kernel,
        grid=(depth, n // tn),
        in_specs=[
            pl.BlockSpec((m, d), lambda l, j: (0, 0)),
            pl.BlockSpec((None, d, tn), lambda l, j: (l, 0, j)),
            pl.BlockSpec((None, 1, tn), lambda l, j: (l, 0, j)),
        ],
        out_specs=pl.BlockSpec((None, m, tn), lambda l, j: (l, 0, j)),
        out_shape=jax.ShapeDtypeStruct((depth, m, n), F32),
        compiler_params=pltpu.CompilerParams(
            dimension_semantics=("arbitrary", "arbitrary")),
        name="ada",
    )(c_all, w_ada, b_ada.reshape(depth, 1, n))


def _s5_disc_kernel(a_re_ref, a_im_ref, log_dt_ref, b_re_ref, b_im_ref,
                    abr_ref, abi_ref, bbr_ref, bbi_ref):
    a_re = a_re_ref[...]
    a_im = a_im_ref[...]
    dt = jnp.exp(log_dt_ref[...])
    mag = jnp.exp(a_re * dt)
    abr = mag * jnp.cos(a_im * dt)
    abi = mag * jnp.sin(a_im * dt)
    nr, ni = abr - 1.0, abi
    den = a_re * a_re + a_im * a_im
    fr = (nr * a_re + ni * a_im) / den
    fi = (ni * a_re - nr * a_im) / den
    b_re = b_re_ref[...]
    b_im = b_im_ref[...]
    abr_ref[...] = abr
    abi_ref[...] = abi
    bbr_ref[...] = fr * b_re - fi * b_im
    bbi_ref[...] = fr * b_im + fi * b_re


def _s5_discretise(a_re, a_im, log_dt, b_re, b_im):
    depth, g, p, q = b_re.shape
    shape2 = (depth * g, p * q)
    bc = lambda a: jnp.broadcast_to(a[..., None], (depth, g, p, q)).reshape(shape2)
    dt_b = jnp.broadcast_to(log_dt[..., None, None], (depth, g, p, q)).reshape(shape2)
    outs = pl.pallas_call(
        _s5_disc_kernel,
        out_shape=[jax.ShapeDtypeStruct(shape2, F32)] * 4,
        name="s5_discretise",
    )(bc(a_re), bc(a_im), dt_b, b_re.reshape(shape2), b_im.reshape(shape2))
    abr, abi, bbr, bbi = (o.reshape(depth, g, p, q) for o in outs)
    return abr[..., 0], abi[..., 0], bbr, bbi


def _scan_slabs(tb, nb, body, carry):
    if tb <= 64:
        for t in range(tb):
            carry = body(t * nb, carry)
        return carry

    def step(t, c):
        return body(pl.multiple_of(t * nb, nb), c)

    return lax.fori_loop(0, tb, step, carry, unroll=8)


def _mixer_kernel(tb, nb, conv_w, n_gate_tiles, n_s5_chunks, alpha,
                  x_ref, ada_ref, h0_ref, conv0_ref, sr0_ref, si0_ref,
                  w_in_ref, b_in_ref, w_conv_ref, b_conv_ref, wg_ref, bg_ref,
                  lam_ref, w_lo_ref, ar_ref, ai_ref, wb_ref, wc_ref, d_ref,
                  w_glu_ref, w_out_ref, lng_ref, lnb_ref,
                  y_ref, h_ref, conv_ref, sr_ref, si_ref,
                  zfull, a_buf, b_buf, s5re, s5im, *relayout):
    r, d = y_ref.shape
    tail = (conv_w - 1) * nb
    s5k, s5w = wb_ref.shape[2:]

    @pl.when(pl.program_id(0) == 0)
    def _():
        h_ref[...] = h0_ref[...]
        zfull[0:tail, :] = conv0_ref[...]
        sr_ref[...] = sr0_ref[...]
        si_ref[...] = si0_ref[...]

    x = _load_time_major(x_ref, *relayout) if relayout else x_ref[...]
    x3 = x.reshape(tb, nb, d)
    ada = lambda k: ada_ref[:, k * d:(k + 1) * d][None]
    u = x3 * (1.0 + ada(1)) + ada(0)
    ub = u.reshape(r, d).astype(BF16)

    def proj(k):
        return _dot(ub, w_in_ref[:, k * d:(k + 1) * d]) + b_in_ref[:, k * d:(k + 1) * d]

    zfull[tail:, :] = proj(0)
    v = b_conv_ref[...]
    for k in range(conv_w):
        v = v + w_conv_ref[k:k + 1, :] * zfull[k * nb:k * nb + r, :]
    new_tail = zfull[r:r + tail, :]
    zfull[0:tail, :] = new_tail
    conv_ref[...] = new_tail

    vb = v.astype(BF16)
    gw = d // n_gate_tiles

    def gate(gi):
        cols = [_dot(vb[:, j * gw:(j + 1) * gw], wg_ref[gi, j]) for j in range(n_gate_tiles)]
        return jax.nn.sigmoid(jnp.concatenate(cols, axis=-1) + bg_ref[gi:gi + 1, :])

    log_a = (-LRU_C) * gate(0) * _softplus(-lam_ref[...])
    a = jnp.exp(log_a)
    b_buf[...] = jnp.sqrt(1.0 - a * a) * (gate(1) * v)
    a_buf[...] = a

    def lru_step(row, h):
        h = a_buf[pl.ds(row, nb), :] * h + b_buf[pl.ds(row, nb), :]
        b_buf[pl.ds(row, nb), :] = h
        return h

    h_ref[...] = _scan_slabs(tb, nb, lru_step, h_ref[...])
    br_a = _dot((b_buf[...] * jax.nn.gelu(proj(1))).astype(BF16), w_lo_ref[...])

    z_s5 = proj(2)
    zsb = z_s5.astype(BF16)
    ys = []
    for c in range(n_s5_chunks):
        cols = slice(c * s5w, (c + 1) * s5w)
        zc = zsb[:, c * s5k:(c + 1) * s5k]
        s5re[...] = _dot(zc, wb_ref[0, c])
        s5im[...] = _dot(zc, wb_ref[1, c])
        ar = jnp.broadcast_to(ar_ref[:, cols], (nb, s5w))
        ai = jnp.broadcast_to(ai_ref[:, cols], (nb, s5w))

        def s5_step(row, carry, ar=ar, ai=ai):
            xr, xi = carry
            rows = pl.ds(row, nb)
            nxr = ar * xr - ai * xi + s5re[rows, :]
            nxi = ar * xi + ai * xr + s5im[rows, :]
            s5re[rows, :] = nxr
            s5im[rows, :] = nxi
            return nxr, nxi

        xr, xi = _scan_slabs(tb, nb, s5_step, (sr_ref[:, cols], si_ref[:, cols]))
        sr_ref[:, cols] = xr
        si_ref[:, cols] = xi
        ys.append(_dot(s5re[...].astype(BF16), wc_ref[0, c])
                  + _dot(s5im[...].astype(BF16), wc_ref[1, c]))
    ys = jnp.concatenate(ys, axis=-1) + d_ref[...] * z_s5
    gl = _dot(jax.nn.gelu(ys).astype(BF16), w_glu_ref[...])
    br_b = gl[:, :d] * jax.nn.sigmoid(gl[:, d:])

    mixed = jax.nn.sigmoid(proj(3)) * br_a + jax.nn.sigmoid(proj(4)) * br_b
    mix = _dot(mixed.astype(BF16), w_out_ref[...]).reshape(tb, nb, d)
    res = alpha * x3 + ada(2) * mix
    y_ref[...] = _layer_norm(res, lng_ref[...], lnb_ref[...]).reshape(r, d)


_MIXER_WEIGHTS = ("w_in", "b_in", "w_conv", "b_conv", "wg", "bg", "lam", "w_lo", "ar", "ai",
                  "wb", "wc", "s5_d", "w_glu", "w_out", "ln1_g", "ln1_b")
_FFN_WEIGHTS = ("w_up", "w_down", "ln2_g", "ln2_b")


def _mixer_call(x, ada, ada_row_block, states, state_layer, lw, layer, *, nb, alpha, casts=()):
    batch_major = x.ndim == 3
    d = x.shape[-1]
    t_total = x.shape[1] if batch_major else x.shape[0] // nb
    rows = t_total * nb
    tb = min(t_total, ROWS_PER_BLOCK // nb)
    r = tb * nb
    conv_w = lw["w_conv"].shape[1]
    n_gate_tiles = lw["wg"].shape[2]
    n_chunks, _, s5w = lw["wb"].shape[2:]
    n_state = states[2].shape[2]
    tail = (conv_w - 1) * nb

    weights = [lw[k] for k in _MIXER_WEIGHTS]
    row_spec = pl.BlockSpec((r, d), lambda i: (i, 0))
    x_spec = pl.BlockSpec((nb, tb, d), lambda i: (0, i, 0)) if batch_major else row_spec
    relayout = [pltpu.VMEM((d // LANES, r, LANES), F32)] if batch_major else []
    n_steps = t_total // tb
    cast_in, cast_out, cast_shape = zip(*[_cast_specs(w, l, n_steps) for w, l in casts]) if casts else ((), (), ())
    body = functools.partial(_mixer_kernel, tb, nb, conv_w, n_gate_tiles, n_chunks, alpha)
    return pl.pallas_call(
        _with_casts(body, 6 + len(weights), 5, len(casts)),
        grid=(n_steps,),
        in_specs=([x_spec, _resident(ada, layer, rows=nb, row_block=ada_row_block)]
                  + [_resident(a, state_layer) for a in states]
                  + [_resident(a, layer) for a in weights] + list(cast_in)),
        out_specs=[row_spec, _carried((nb, d)), _carried((tail, d)),
                   _carried((nb, n_state)), _carried((nb, n_state))] + list(cast_out),
        out_shape=[jax.ShapeDtypeStruct((rows, d), F32),
                   jax.ShapeDtypeStruct((nb, d), F32),
                   jax.ShapeDtypeStruct((tail, d), F32),
                   jax.ShapeDtypeStruct((nb, n_state), F32),
                   jax.ShapeDtypeStruct((nb, n_state), F32)] + list(cast_shape),
        scratch_shapes=[pltpu.VMEM((r + tail, d), F32),
                        pltpu.VMEM((r, d), F32),
                        pltpu.VMEM((r, d), F32),
                        pltpu.VMEM((r, s5w), F32),
                        pltpu.VMEM((r, s5w), F32)] + relayout,
        compiler_params=pltpu.CompilerParams(
            dimension_semantics=("arbitrary",), vmem_limit_bytes=VMEM_LIMIT),
        name="mixer",
    )(x, ada, *states, *[w[layer] if isinstance(w, list) else w for w in weights],
      *[w for w, _ in casts])


def _ffn_kernel(tb, nb, alpha, x_ref, ada_ref, w_up_ref, w_down_ref, lng_ref, lnb_ref, y_ref,
                *relayout):
    r, d = x_ref.shape
    d_ff = w_down_ref.shape[0]
    x3 = x_ref[...].reshape(tb, nb, d)
    ada = lambda k: ada_ref[:, k * d:(k + 1) * d][None]
    u = (x3 * (1.0 + ada(4)) + ada(3)).reshape(r, d).astype(BF16)
    hf = _dot(u, w_up_ref[...])
    act = (jax.nn.silu(hf[:, :d_ff]) * hf[:, d_ff:]).astype(BF16)
    f = _dot(act, w_down_ref[...]).reshape(tb, nb, d)
    res = alpha * x3 + ada(5) * f
    y = _layer_norm(res, lng_ref[...], lnb_ref[...]).reshape(r, d)
    if relayout:
        _store_batch_major(y, y_ref, *relayout)
    else:
        y_ref[...] = y


def _ffn_call(x, ada, ada_row_block, lw, layer, *, nb, alpha, out_batch_major=False, casts=()):
    rows, d = x.shape
    t_total = rows // nb
    tb = min(t_total, ROWS_PER_BLOCK // nb)
    r = tb * nb
    weights = [lw[k] for k in _FFN_WEIGHTS]
    row_spec = pl.BlockSpec((r, d), lambda i: (i, 0))
    n_steps = t_total // tb
    cast_in, cast_out, cast_shape = zip(*[_cast_specs(w, l, n_steps) for w, l in casts]) if casts else ((), (), ())
    y_spec = pl.BlockSpec((nb, tb, d), lambda i: (0, i, 0)) if out_batch_major else row_spec
    y_shape = jax.ShapeDtypeStruct((nb, t_total, d) if out_batch_major else (rows, d), F32)
    return pl.pallas_call(
        _with_casts(functools.partial(_ffn_kernel, tb, nb, alpha), 2 + len(weights), 1, len(casts)),
        grid=(n_steps,),
        in_specs=([row_spec, _resident(ada, layer, rows=nb, row_block=ada_row_block)]
                  + [_resident(a, layer) for a in weights] + list(cast_in)),
        out_specs=[y_spec] + list(cast_out),
        out_shape=[y_shape] + list(cast_shape),
        scratch_shapes=[pltpu.VMEM((d // LANES, r, LANES), F32)] if out_batch_major else [],
        compiler_params=pltpu.CompilerParams(
            dimension_semantics=("arbitrary",), vmem_limit_bytes=VMEM_LIMIT),
        name="ffn",
    )(x, ada, *[w[layer] if isinstance(w, list) else w for w in weights], *[w for w, _ in casts])


def _block_diag(x, n):
    rows, b = x.shape[-2:]
    tiled = jnp.tile(x, (1,) * (x.ndim - 1) + (n,))
    rb = lax.broadcasted_iota(jnp.int32, (rows, n * b), 0) // (rows // n)
    cb = lax.broadcasted_iota(jnp.int32, (rows, n * b), 1) // b
    return jnp.where(rb == cb, tiled, 0.0)


def _pack_weights(p, abr, abi, bbr, bbi):
    depth = p["w_in"].shape[0]
    row = lambda a: a.reshape(depth, 1, -1)
    wgl = p["w_lru_gates"]
    heads, bh = wgl.shape[2], wgl.shape[3]
    hpt = max(1, MXU_DIM // bh)
    wg = _block_diag(wgl.reshape(depth, 2, heads // hpt, hpt * bh, bh), hpt)
    g_n, _, q_n = bbr.shape[1:]
    gpc = MXU_DIM // q_n

    def packed(re, im):
        a = jnp.stack([re, im], axis=1)
        m, k = a.shape[3:]
        at = jnp.swapaxes(a, -1, -2).reshape(depth, 2, g_n // gpc, gpc * k, m)
        return _block_diag(at, gpc)

    wb = packed(bbr, bbi)
    wc = packed(p["s5_c_re"], -p["s5_c_im"])
    return dict(
        b_in=row(p["b_in"]),
        w_conv=p["w_conv"], b_conv=row(p["b_conv"]),
        wg=wg.astype(BF16), bg=p["b_lru_gates"], lam=row(p["lru_lambda"]),
        ar=row(abr), ai=row(abi), wb=wb.astype(BF16), wc=wc.astype(BF16),
        s5_d=row(p["s5_d"]),
        ln1_g=row(p["ln1_g"]), ln1_b=row(p["ln1_b"]),
        ln2_g=row(p["ln2_g"]), ln2_b=row(p["ln2_b"]),
    )


def _time_major(x):
    *lead, b, t, d = x.shape
    return jnp.swapaxes(x, -3, -2).reshape(*lead, t * b, d)


def _batch_major(x, b):
    *lead, rows, d = x.shape
    return jnp.swapaxes(x.reshape(*lead, rows // b, b, d), -3, -2)


def kernel(x_prompt, x_sample, c_prompt, c_sample, state_lru_h, state_lru_conv, state_s5_re, state_s5_im, w_ada, b_ada, w_in, b_in, w_conv, b_conv, w_lru_gates, b_lru_gates, lru_lambda, w_lru_out, s5_a_re, s5_a_im, s5_log_dt, s5_b_re, s5_b_im, s5_c_re, s5_c_im, s5_d, w_s5_glu, w_out, ln1_g, ln1_b, w_ffn_up, w_ffn_down, ln2_g, ln2_b):
    p = dict(w_in=w_in, b_in=b_in, w_conv=w_conv, b_conv=b_conv, w_lru_gates=w_lru_gates,
             b_lru_gates=b_lru_gates, lru_lambda=lru_lambda, w_lru_out=w_lru_out,
             s5_c_re=s5_c_re, s5_c_im=s5_c_im, s5_d=s5_d, w_s5_glu=w_s5_glu, w_out=w_out,
             ln1_g=ln1_g, ln1_b=ln1_b, w_ffn_up=w_ffn_up, w_ffn_down=w_ffn_down,
             ln2_g=ln2_g, ln2_b=ln2_b)
    depth = w_ada.shape[0]
    bp, _, d = x_prompt.shape
    bs = x_sample.shape[0]
    conv_w = w_conv.shape[1]
    sshape = s5_a_re.shape[1:]
    n_state = sshape[0] * sshape[1]
    alpha = float((2 * depth) ** 0.25)

    ada_all = _ada_call(jnp.concatenate([c_sample, c_prompt], axis=0), w_ada, b_ada)
    abr, abi, bbr, bbi = _s5_discretise(s5_a_re, s5_a_im, s5_log_dt, s5_b_re, s5_b_im)
    lw = _pack_weights(p, abr, abi, bbr, bbi)
    dense = dict(w_in=w_in, w_lo=w_lru_out, w_glu=w_s5_glu, w_out=w_out, w_up=w_ffn_up, w_down=w_ffn_down)
    mixer_dense, ffn_dense = ("w_in", "w_lo", "w_glu", "w_out"), ("w_up", "w_down")
    for k in dense:
        lw[k] = [None] * depth
    for k in mixer_dense:
        lw[k][0] = dense[k][0].astype(BF16)

    states_s = (state_lru_h, _time_major(state_lru_conv),
                state_s5_re.reshape(depth, bs, n_state), state_s5_im.reshape(depth, bs, n_state))
    states_p = (jnp.zeros((1, bp, d), F32), jnp.zeros((1, (conv_w - 1) * bp, d), F32),
                jnp.zeros((1, bp, n_state), F32), jnp.zeros((1, bp, n_state), F32))

    xp, xs = x_prompt, _time_major(x_sample)
    outs_p, outs_s = [], []
    for l in range(depth):
        xp, *rest = _mixer_call(xp, ada_all, bs // bp, states_p, 0, lw, l, nb=bp, alpha=alpha,
                                casts=[(dense[k], l) for k in ffn_dense])
        st_p, (lw["w_up"][l], lw["w_down"][l]) = rest[:4], rest[4:]
        nxt = [(dense[k], l + 1) for k in mixer_dense] if l + 1 < depth else []
        xp, *cast = _ffn_call(xp, ada_all, bs // bp, lw, l, nb=bp, alpha=alpha,
                              out_batch_major=(l == depth - 1), casts=nxt)
        for k, w in zip(mixer_dense, cast):
            lw[k][l + 1] = w
        xs, *st_s = _mixer_call(xs, ada_all, 0, states_s, l, lw, l, nb=bs, alpha=alpha)
        xs, = _ffn_call(xs, ada_all, 0, lw, l, nb=bs, alpha=alpha)
        outs_p.append(st_p)
        outs_s.append(st_s)

    def finish(outs, b):
        h, cv, sr, si = (jnp.stack(o) for o in zip(*outs))
        return (h, _batch_major(cv, b), sr.reshape(depth, b, *sshape), si.reshape(depth, b, *sshape))

    hp, cvp, srp, sip = finish(outs_p, bp)
    hs, cvs, srs, sis = finish(outs_s, bs)
    return (xp, _batch_major(xs, bs), hp, hs, cvp, cvs, srp, srs, sip, sis)
```

```python
import functools

import jax
import jax.numpy as jnp
from jax import lax
from jax.experimental import pallas as pl
from jax.experimental.pallas import tpu as pltpu

LRU_C = 8.0
LN_EPS = 1e-5
N_ADA = 6
F32 = jnp.float32
BF16 = jnp.bfloat16

LANES = 128
BF16_ROWS = 16
MXU_DIM = 256
ROWS_PER_BLOCK = 512
VMEM_LIMIT = 60 * 1024 * 1024


def _resident(arr, layer, rows=None, row_block=0):
    if isinstance(arr, list):
        shape = arr[layer].shape
        return pl.BlockSpec(shape, lambda *_: (0,) * len(shape), pipeline_mode=pl.Buffered(1))
    shape = list(arr.shape[1:])
    if rows is not None:
        shape[0] = rows
    idx = (layer, row_block) + (0,) * (len(shape) - 1)
    return pl.BlockSpec((None, *shape), lambda *_: idx, pipeline_mode=pl.Buffered(1))


def _carried(shape):
    nd = len(shape)
    return pl.BlockSpec(shape, lambda *_: (0,) * nd)


def _with_casts(body, n_in, n_out, n_cast):
    def kern(*refs):
        ins, refs = refs[:n_in], refs[n_in:]
        cast_in, refs = refs[:n_cast], refs[n_cast:]
        outs, refs = refs[:n_out], refs[n_out:]
        cast_out, scratch = refs[:n_cast], refs[n_cast:]
        for src, dst in zip(cast_in, cast_out):
            dst[...] = src[...].astype(BF16)
        body(*ins, *outs, *scratch)
    return kern


def _cast_specs(w, layer, n_steps):
    rows, cols = w.shape[1:]
    need = -(-rows // n_steps)
    rb = next(b for b in range(BF16_ROWS, rows + 1, BF16_ROWS) if rows % b == 0 and b >= need)
    last = rows // rb - 1
    return (pl.BlockSpec((None, rb, cols), lambda i: (layer, jnp.minimum(i, last), 0)),
            pl.BlockSpec((rb, cols), lambda i: (jnp.minimum(i, last), 0)),
            jax.ShapeDtypeStruct((rows, cols), BF16))


def _load_time_major(x_ref, relayout_ref):
    nb, tb, d = x_ref.shape
    for j in range(d // LANES):
        for b in range(nb):
            relayout_ref[j, pl.ds(b, tb, stride=nb), :] = x_ref[b, :, j * LANES:(j + 1) * LANES]
    return jnp.concatenate([relayout_ref[j] for j in range(d // LANES)], axis=-1)


def _store_batch_major(y, y_ref, relayout_ref):
    nb, tb, d = y_ref.shape
    for j in range(d // LANES):
        relayout_ref[j] = y[:, j * LANES:(j + 1) * LANES]
    for j in range(d // LANES):
        for b in range(nb):
            y_ref[b, :, j * LANES:(j + 1) * LANES] = relayout_ref[j, pl.ds(b, tb, stride=nb), :]


def _layer_norm(x, g, b):
    mu = jnp.mean(x, axis=-1, keepdims=True)
    xc = x - mu
    var = jnp.mean(xc * xc, axis=-1, keepdims=True)
    return xc * lax.rsqrt(var + LN_EPS) * g + b


def _sigmoid(x):
    return 0.5 * jnp.tanh(0.5 * x) + 0.5


def _silu(x):
    h = 0.5 * x
    return h + h * jnp.tanh(h)


def _sqrt_nonneg(x):
    return jnp.where(x > 0.0, x * lax.rsqrt(x), 0.0)


def _softplus(x):
    return jnp.maximum(x, 0.0) + jnp.log1p(jnp.exp(-jnp.abs(x)))


def _dot(a, b):
    return jnp.dot(a, b, preferred_element_type=F32)


def _ada_kernel(c_ref, w_ref, b_ref, o_ref):
    s = jax.nn.silu(c_ref[...]).astype(BF16)
    o_ref[...] = _dot(s, w_ref[...].astype(BF16)) + b_ref[...]


def _ada_call(c_all, w_ada, b_ada):
    depth, d, n = w_ada.shape
    m = c_all.shape[0]
    tn = 2 * d if n % (2 * d) == 0 else d
    return pl.pallas_call(
        _ada_kernel,
        grid=(depth, n // tn),
        in_specs=[
            pl.BlockSpec((m, d), lambda l, j: (0, 0)),
            pl.BlockSpec((None, d, tn), lambda l, j: (l, 0, j)),
            pl.BlockSpec((None, 1, tn), lambda l, j: (l, 0, j)),
        ],
        out_specs=pl.BlockSpec((None, m, tn), lambda l, j: (l, 0, j)),
        out_shape=jax.ShapeDtypeStruct((depth, m, n), F32),
        compiler_params=pltpu.CompilerParams(
            dimension_semantics=("arbitrary", "arbitrary")),
        name="ada",
    )(c_all, w_ada, b_ada.reshape(depth, 1, n))


def _s5_disc_kernel(a_re_ref, a_im_ref, log_dt_ref, b_re_ref, b_im_ref,
                    abr_ref, abi_ref, bbr_ref, bbi_ref):
    a_re = a_re_ref[...]
    a_im = a_im_ref[...]
    dt = jnp.exp(log_dt_ref[...])
    mag = jnp.exp(a_re * dt)
    abr = mag * jnp.cos(a_im * dt)
    abi = mag * jnp.sin(a_im * dt)
    nr, ni = abr - 1.0, abi
    den = a_re * a_re + a_im * a_im
    fr = (nr * a_re + ni * a_im) / den
    fi = (ni * a_re - nr * a_im) / den
    b_re = b_re_ref[...]
    b_im = b_im_ref[...]
    abr_ref[...] = abr
    abi_ref[...] = abi
    bbr_ref[...] = fr * b_re - fi * b_im
    bbi_ref[...] = fr * b_im + fi * b_re


def _s5_discretise(a_re, a_im, log_dt, b_re, b_im):
    depth, g, p, q = b_re.shape
    shape2 = (depth * g, p * q)
    bc = lambda a: jnp.broadcast_to(a[..., None], (depth, g, p, q)).reshape(shape2)
    dt_b = jnp.broadcast_to(log_dt[..., None, None], (depth, g, p, q)).reshape(shape2)
    outs = pl.pallas_call(
        _s5_disc_kernel,
        out_shape=[jax.ShapeDtypeStruct(shape2, F32)] * 4,
        name="s5_discretise",
    )(bc(a_re), bc(a_im), dt_b, b_re.reshape(shape2), b_im.reshape(shape2))
    abr, abi, bbr, bbi = (o.reshape(depth, g, p, q) for o in outs)
    return abr[..., 0], abi[..., 0], bbr, bbi


def _scan_slabs(tb, nb, body, carry):
    if tb <= 64:
        for t in range(tb):
            carry = body(t * nb, carry)
        return carry

    def step(t, c):
        return body(pl.multiple_of(t * nb, nb), c)

    return lax.fori_loop(0, tb, step, carry, unroll=8)


def _mixer_kernel(tb, nb, conv_w, n_gate_tiles, n_s5_chunks, alpha,
                  x_ref, ada_ref, h0_ref, conv0_ref, sr0_ref, si0_ref,
                  w_in_ref, b_in_ref, w_conv_ref, b_conv_ref, wg_ref, bg_ref,
                  lam_ref, w_lo_ref, ar_ref, ai_ref, wb_ref, wc_ref, d_ref,
                  w_glu_ref, w_out_ref, lng_ref, lnb_ref,
                  y_ref, h_ref, conv_ref, sr_ref, si_ref,
                  zfull, a_buf, b_buf, s5re, s5im, *relayout):
    r, d = y_ref.shape
    tail = (conv_w - 1) * nb
    s5k, s5w = wb_ref.shape[2:]

    @pl.when(pl.program_id(0) == 0)
    def _():
        h_ref[...] = h0_ref[...]
        zfull[0:tail, :] = conv0_ref[...]
        sr_ref[...] = sr0_ref[...]
        si_ref[...] = si0_ref[...]

    x = _load_time_major(x_ref, *relayout) if relayout else x_ref[...]
    x3 = x.reshape(tb, nb, d)
    ada = lambda k: ada_ref[:, k * d:(k + 1) * d][None]
    u = x3 * (1.0 + ada(1)) + ada(0)
    ub = u.reshape(r, d).astype(BF16)

    def proj(k):
        return _dot(ub, w_in_ref[:, k * d:(k + 1) * d]) + b_in_ref[:, k * d:(k + 1) * d]

    zfull[tail:, :] = proj(0)
    v = b_conv_ref[...]
    for k in range(conv_w):
        v = v + w_conv_ref[k:k + 1, :] * zfull[k * nb:k * nb + r, :]
    new_tail = zfull[r:r + tail, :]
    zfull[0:tail, :] = new_tail
    conv_ref[...] = new_tail

    vb = v.astype(BF16)
    gw = d // n_gate_tiles

    def gate(gi):
        cols = [_dot(vb[:, j * gw:(j + 1) * gw], wg_ref[gi, j]) for j in range(n_gate_tiles)]
        return _sigmoid(jnp.concatenate(cols, axis=-1) + bg_ref[gi:gi + 1, :])

    log_a = (-LRU_C) * gate(0) * _softplus(-lam_ref[...])
    a = jnp.exp(log_a)
    b_buf[...] = _sqrt_nonneg(1.0 - a * a) * (gate(1) * v)
    a_buf[...] = a

    def lru_step(row, h):
        h = a_buf[pl.ds(row, nb), :] * h + b_buf[pl.ds(row, nb), :]
        b_buf[pl.ds(row, nb), :] = h
        return h

    h_ref[...] = _scan_slabs(tb, nb, lru_step, h_ref[...])
    br_a = _dot((b_buf[...] * jax.nn.gelu(proj(1))).astype(BF16), w_lo_ref[...])

    z_s5 = proj(2)
    zsb = z_s5.astype(BF16)
    ys = []
    for c in range(n_s5_chunks):
        cols = slice(c * s5w, (c + 1) * s5w)
        zc = zsb[:, c * s5k:(c + 1) * s5k]
        s5re[...] = _dot(zc, wb_ref[0, c])
        s5im[...] = _dot(zc, wb_ref[1, c])
        ar = jnp.broadcast_to(ar_ref[:, cols], (nb, s5w))
        ai = jnp.broadcast_to(ai_ref[:, cols], (nb, s5w))

        def s5_step(row, carry, ar=ar, ai=ai):
            xr, xi = carry
            rows = pl.ds(row, nb)
            nxr = ar * xr - ai * xi + s5re[rows, :]
            nxi = ar * xi + ai * xr + s5im[rows, :]
            s5re[rows, :] = nxr
            s5im[rows, :] = nxi
            return nxr, nxi

        xr, xi = _scan_slabs(tb, nb, s5_step, (sr_ref[:, cols], si_ref[:, cols]))
        sr_ref[:, cols] = xr
        si_ref[:, cols] = xi
        ys.append(_dot(s5re[...].astype(BF16), wc_ref[0, c])
                  + _dot(s5im[...].astype(BF16), wc_ref[1, c]))
    ys = jnp.concatenate(ys, axis=-1) + d_ref[...] * z_s5
    gl = _dot(jax.nn.gelu(ys).astype(BF16), w_glu_ref[...])
    br_b = gl[:, :d] * _sigmoid(gl[:, d:])

    mixed = _sigmoid(proj(3)) * br_a + _sigmoid(proj(4)) * br_b
    mix = _dot(mixed.astype(BF16), w_out_ref[...]).reshape(tb, nb, d)
    res = alpha * x3 + ada(2) * mix
    y_ref[...] = _layer_norm(res, lng_ref[...], lnb_ref[...]).reshape(r, d)


_MIXER_WEIGHTS = ("w_in", "b_in", "w_conv", "b_conv", "wg", "bg", "lam", "w_lo", "ar", "ai",
                  "wb", "wc", "s5_d", "w_glu", "w_out", "ln1_g", "ln1_b")
_FFN_WEIGHTS = ("w_up", "w_down", "ln2_g", "ln2_b")


def _mixer_call(x, ada, ada_row_block, states, state_layer, lw, layer, *, nb, alpha, casts=()):
    batch_major = x.ndim == 3
    d = x.shape[-1]
    t_total = x.shape[1] if batch_major else x.shape[0] // nb
    rows = t_total * nb
    tb = min(t_total, ROWS_PER_BLOCK // nb)
    r = tb * nb
    conv_w = lw["w_conv"].shape[1]
    n_gate_tiles = lw["wg"].shape[2]
    n_chunks, _, s5w = lw["wb"].shape[2:]
    n_state = states[2].shape[2]
    tail = (conv_w - 1) * nb

    weights = [lw[k] for k in _MIXER_WEIGHTS]
    row_spec = pl.BlockSpec((r, d), lambda i: (i, 0))
    x_spec = pl.BlockSpec((nb, tb, d), lambda i: (0, i, 0)) if batch_major else row_spec
    relayout = [pltpu.VMEM((d // LANES, r, LANES), F32)] if batch_major else []
    n_steps = t_total // tb
    cast_in, cast_out, cast_shape = zip(*[_cast_specs(w, l, n_steps) for w, l in casts]) if casts else ((), (), ())
    body = functools.partial(_mixer_kernel, tb, nb, conv_w, n_gate_tiles, n_chunks, alpha)
    return pl.pallas_call(
        _with_casts(body, 6 + len(weights), 5, len(casts)),
        grid=(n_steps,),
        in_specs=([x_spec, _resident(ada, layer, rows=nb, row_block=ada_row_block)]
                  + [_resident(a, state_layer) for a in states]
                  + [_resident(a, layer) for a in weights] + list(cast_in)),
        out_specs=[row_spec, _carried((nb, d)), _carried((tail, d)),
                   _carried((nb, n_state)), _carried((nb, n_state))] + list(cast_out),
        out_shape=[jax.ShapeDtypeStruct((rows, d), F32),
                   jax.ShapeDtypeStruct((nb, d), F32),
                   jax.ShapeDtypeStruct((tail, d), F32),
                   jax.ShapeDtypeStruct((nb, n_state), F32),
                   jax.ShapeDtypeStruct((nb, n_state), F32)] + list(cast_shape),
        scratch_shapes=[pltpu.VMEM((r + tail, d), F32),
                        pltpu.VMEM((r, d), F32),
                        pltpu.VMEM((r, d), F32),
                        pltpu.VMEM((r, s5w), F32),
                        pltpu.VMEM((r, s5w), F32)] + relayout,
        compiler_params=pltpu.CompilerParams(
            dimension_semantics=("arbitrary",), vmem_limit_bytes=VMEM_LIMIT),
        name="mixer",
    )(x, ada, *states, *[w[layer] if isinstance(w, list) else w for w in weights],
      *[w for w, _ in casts])


def _ffn_kernel(tb, nb, alpha, x_ref, ada_ref, w_up_ref, w_down_ref, lng_ref, lnb_ref, y_ref,
                *relayout):
    r, d = x_ref.shape
    d_ff = w_down_ref.shape[0]
    x3 = x_ref[...].reshape(tb, nb, d)
    ada = lambda k: ada_ref[:, k * d:(k + 1) * d][None]
    u = (x3 * (1.0 + ada(4)) + ada(3)).reshape(r, d).astype(BF16)
    hf = _dot(u, w_up_ref[...])
    act = (_silu(hf[:, :d_ff]) * hf[:, d_ff:]).astype(BF16)
    f = _dot(act, w_down_ref[...]).reshape(tb, nb, d)
    res = alpha * x3 + ada(5) * f
    y = _layer_norm(res, lng_ref[...], lnb_ref[...]).reshape(r, d)
    if relayout:
        _store_batch_major(y, y_ref, *relayout)
    else:
        y_ref[...] = y


def _ffn_call(x, ada, ada_row_block, lw, layer, *, nb, alpha, out_batch_major=False, casts=()):
    rows, d = x.shape
    t_total = rows // nb
    tb = min(t_total, ROWS_PER_BLOCK // nb)
    r = tb * nb
    weights = [lw[k] for k in _FFN_WEIGHTS]
    row_spec = pl.BlockSpec((r, d), lambda i: (i, 0))
    n_steps = t_total // tb
    cast_in, cast_out, cast_shape = zip(*[_cast_specs(w, l, n_steps) for w, l in casts]) if casts else ((), (), ())
    y_spec = pl.BlockSpec((nb, tb, d), lambda i: (0, i, 0)) if out_batch_major else row_spec
    y_shape = jax.ShapeDtypeStruct((nb, t_total, d) if out_batch_major else (rows, d), F32)
    return pl.pallas_call(
        _with_casts(functools.partial(_ffn_kernel, tb, nb, alpha), 2 + len(weights), 1, len(casts)),
        grid=(n_steps,),
        in_specs=([row_spec, _resident(ada, layer, rows=nb, row_block=ada_row_block)]
                  + [_resident(a, layer) for a in weights] + list(cast_in)),
        out_specs=[y_spec] + list(cast_out),
        out_shape=[y_shape] + list(cast_shape),
        scratch_shapes=[pltpu.VMEM((d // LANES, r, LANES), F32)] if out_batch_major else [],
        compiler_params=pltpu.CompilerParams(
            dimension_semantics=("arbitrary",), vmem_limit_bytes=VMEM_LIMIT),
        name="ffn",
    )(x, ada, *[w[layer] if isinstance(w, list) else w for w in weights], *[w for w, _ in casts])


def _block_diag(x, n):
    rows, b = x.shape[-2:]
    tiled = jnp.tile(x, (1,) * (x.ndim - 1) + (n,))
    rb = lax.broadcasted_iota(jnp.int32, (rows, n * b), 0) // (rows // n)
    cb = lax.broadcasted_iota(jnp.int32, (rows, n * b), 1) // b
    return jnp.where(rb == cb, tiled, 0.0)


def _pack_weights(p, abr, abi, bbr, bbi):
    depth = p["w_in"].shape[0]
    row = lambda a: a.reshape(depth, 1, -1)
    wgl = p["w_lru_gates"]
    heads, bh = wgl.shape[2], wgl.shape[3]
    hpt = max(1, MXU_DIM // bh)
    wg = _block_diag(wgl.reshape(depth, 2, heads // hpt, hpt * bh, bh), hpt)
    g_n, _, q_n = bbr.shape[1:]
    gpc = MXU_DIM // q_n

    def packed(re, im):
        a = jnp.stack([re, im], axis=1)
        m, k = a.shape[3:]
        at = jnp.swapaxes(a, -1, -2).reshape(depth, 2, g_n // gpc, gpc * k, m)
        return _block_diag(at, gpc)

    wb = packed(bbr, bbi)
    wc = packed(p["s5_c_re"], -p["s5_c_im"])
    return dict(
        b_in=row(p["b_in"]),
        w_conv=p["w_conv"], b_conv=row(p["b_conv"]),
        wg=wg.astype(BF16), bg=p["b_lru_gates"], lam=row(p["lru_lambda"]),
        ar=row(abr), ai=row(abi), wb=wb.astype(BF16), wc=wc.astype(BF16),
        s5_d=row(p["s5_d"]),
        ln1_g=row(p["ln1_g"]), ln1_b=row(p["ln1_b"]),
        ln2_g=row(p["ln2_g"]), ln2_b=row(p["ln2_b"]),
    )


def _time_major(x):
    *lead, b, t, d = x.shape
    return jnp.swapaxes(x, -3, -2).reshape(*lead, t * b, d)


def _batch_major(x, b):
    *lead, rows, d = x.shape
    return jnp.swapaxes(x.reshape(*lead, rows // b, b, d), -3, -2)


def kernel(x_prompt, x_sample, c_prompt, c_sample, state_lru_h, state_lru_conv, state_s5_re, state_s5_im, w_ada, b_ada, w_in, b_in, w_conv, b_conv, w_lru_gates, b_lru_gates, lru_lambda, w_lru_out, s5_a_re, s5_a_im, s5_log_dt, s5_b_re, s5_b_im, s5_c_re, s5_c_im, s5_d, w_s5_glu, w_out, ln1_g, ln1_b, w_ffn_up, w_ffn_down, ln2_g, ln2_b):
    p = dict(w_in=w_in, b_in=b_in, w_conv=w_conv, b_conv=b_conv, w_lru_gates=w_lru_gates,
             b_lru_gates=b_lru_gates, lru_lambda=lru_lambda, w_lru_out=w_lru_out,
             s5_c_re=s5_c_re, s5_c_im=s5_c_im, s5_d=s5_d, w_s5_glu=w_s5_glu, w_out=w_out,
             ln1_g=ln1_g, ln1_b=ln1_b, w_ffn_up=w_ffn_up, w_ffn_down=w_ffn_down,
             ln2_g=ln2_g, ln2_b=ln2_b)
    depth = w_ada.shape[0]
    bp, _, d = x_prompt.shape
    bs = x_sample.shape[0]
    conv_w = w_conv.shape[1]
    sshape = s5_a_re.shape[1:]
    n_state = sshape[0] * sshape[1]
    alpha = float((2 * depth) ** 0.25)

    ada_all = _ada_call(jnp.concatenate([c_sample, c_prompt], axis=0), w_ada, b_ada)
    abr, abi, bbr, bbi = _s5_discretise(s5_a_re, s5_a_im, s5_log_dt, s5_b_re, s5_b_im)
    lw = _pack_weights(p, abr, abi, bbr, bbi)
    dense = dict(w_in=w_in, w_lo=w_lru_out, w_glu=w_s5_glu, w_out=w_out, w_up=w_ffn_up, w_down=w_ffn_down)
    mixer_dense, ffn_dense = ("w_in", "w_lo", "w_glu", "w_out"), ("w_up", "w_down")
    for k in dense:
        lw[k] = [None] * depth
    for k in mixer_dense:
        lw[k][0] = dense[k][0].astype(BF16)

    states_s = (state_lru_h, _time_major(state_lru_conv),
                state_s5_re.reshape(depth, bs, n_state), state_s5_im.reshape(depth, bs, n_state))
    states_p = (jnp.zeros((1, bp, d), F32), jnp.zeros((1, (conv_w - 1) * bp, d), F32),
                jnp.zeros((1, bp, n_state), F32), jnp.zeros((1, bp, n_state), F32))

    xp, xs = x_prompt, _time_major(x_sample)
    outs_p, outs_s = [], []
    for l in range(depth):
        xp, *rest = _mixer_call(xp, ada_all, bs // bp, states_p, 0, lw, l, nb=bp, alpha=alpha,
                                casts=[(dense[k], l) for k in ffn_dense])
        st_p, (lw["w_up"][l], lw["w_down"][l]) = rest[:4], rest[4:]
        nxt = [(dense[k], l + 1) for k in mixer_dense] if l + 1 < depth else []
        xp, *cast = _ffn_call(xp, ada_all, bs // bp, lw, l, nb=bp, alpha=alpha,
                              out_batch_major=(l == depth - 1), casts=nxt)
        for k, w in zip(mixer_dense, cast):
            lw[k][l + 1] = w
        xs, *st_s = _mixer_call(xs, ada_all, 0, states_s, l, lw, l, nb=bs, alpha=alpha)
        xs, = _ffn_call(xs, ada_all, 0, lw, l, nb=bs, alpha=alpha)
        outs_p.append(st_p)
        outs_s.append(st_s)

    def finish(outs, b):
        h, cv, sr, si = (jnp.stack(o) for o in zip(*outs))
        return (h, _batch_major(cv, b), sr.reshape(depth, b, *sshape), si.reshape(depth, b, *sshape))

    hp, cvp, srp, sip = finish(outs_p, bp)
    hs, cvs, srs, sis = finish(outs_s, bs)
    return (xp, _batch_major(xs, bs), hp, hs, cvp, cvs, srp, srs, sip, sis)
```
